```python
import math
import jax, jax.numpy as jnp
from jax import lax
import numpy as np

D_MODEL = 2048
BATCH = 16
SEQ = 2048
DEPTH = 1

CHUNK = 64
Q_BLOCK = 128
PLE_DIM = 256
EPS = 1e-6
DN_HEADS = 8
DN_HEAD_DIM = 128
DN_WIDTH = DN_HEADS * DN_HEAD_DIM
CONV_K = 4
MLA_HEADS = 8
MLA_NOPE = 128
MLA_ROPE = 64
MLA_V = 128
KV_RANK = 512
MLA_WIDTH = MLA_HEADS * MLA_V
ROPE_BASE = 10000.0
D_FF = 4 * D_MODEL
IN_SIZES = (DN_WIDTH, DN_WIDTH, DN_WIDTH, DN_WIDTH, DN_HEADS, DN_HEADS,
            MLA_HEADS * (MLA_NOPE + MLA_ROPE), KV_RANK, MLA_ROPE, D_MODEL, D_MODEL)
D_IN = sum(IN_SIZES)

kernel_name = "hybrid_gdn_mla_parallel_block"


def rms_norm(x, g):
    xf = x.astype(jnp.float32)
    y = xf * lax.rsqrt(jnp.mean(xf * xf, axis=-1, keepdims=True) + EPS)
    return (y * g.astype(jnp.float32)).astype(x.dtype)


def l2norm(t):
    t = t.astype(jnp.float32)
    return t * lax.rsqrt(jnp.sum(t * t, axis=-1, keepdims=True) + EPS)


def causal_conv(u, w):
    k, c = w.shape
    return lax.conv_general_dilated(u, w[:, None, :].astype(u.dtype), window_strides=(1,),
                                    padding=[(k - 1, 0)], dimension_numbers=('NWC', 'WIO', 'NWC'),
                                    feature_group_count=c)


def rope(x, pos):
    half = x.shape[-1] // 2
    inv = ROPE_BASE ** (-jnp.arange(half, dtype=jnp.float32) / half)
    ang = pos.astype(jnp.float32)[..., None] * inv
    ang = ang.reshape(ang.shape[:2] + (1,) * (x.ndim - 3) + (half,))
    cos = jnp.cos(ang).astype(x.dtype)
    sin = jnp.sin(ang).astype(x.dtype)
    x1, x2 = x[..., :half], x[..., half:]
    return jnp.concatenate([x1 * cos - x2 * sin, x2 * cos + x1 * sin], axis=-1)


def gated_delta_rule(q, k, v, g, beta):
    b, s, h, dk = q.shape
    dv = v.shape[-1]
    n = s // CHUNK

    def blk(t):
        t = t.reshape((b, n, CHUNK, h) + t.shape[3:])
        return jnp.moveaxis(t, 3, 1)

    q = blk(q) * (dk ** -0.5)
    k, v, g, beta = blk(k), blk(v), blk(g), blk(beta)
    gc = jnp.cumsum(g, axis=-1)
    idx = jnp.arange(CHUNK)
    incl = idx[:, None] >= idx[None, :]
    strict = idx[:, None] > idx[None, :]
    decay = jnp.exp(jnp.where(incl, gc[..., :, None] - gc[..., None, :], -jnp.inf))
    kb = k * beta[..., None]
    vb = v * beta[..., None]
    a = jnp.where(strict, jnp.einsum('bhnid,bhnjd->bhnij', kb, k) * decay, 0.0)
    eye = jnp.eye(CHUNK, dtype=a.dtype)
    t_inv = lax.linalg.triangular_solve(eye + a, jnp.broadcast_to(eye, a.shape),
                                        left_side=True, lower=True)
    w = t_inv @ (kb * jnp.exp(gc)[..., None])
    u = t_inv @ vb
    qk = jnp.einsum('bhnid,bhnjd->bhnij', q, k) * decay
    q_dec = q * jnp.exp(gc)[..., None]
    k_dec = k * jnp.exp(gc[..., -1:] - gc)[..., None]
    g_last = jnp.exp(gc[..., -1])

    def step(state, xs):
        w_n, u_n, q_n, k_n, qk_n, gl_n = xs
        v_new = u_n - w_n @ state
        o = q_n @ state + qk_n @ v_new
        state = state * gl_n[..., None, None] + jnp.einsum('bhcd,bhce->bhde', k_n, v_new)
        return state, o

    xs = tuple(jnp.moveaxis(t_, 2, 0) for t_ in (w, u, q_dec, k_dec, qk, g_last))
    s0 = jnp.zeros((b, h, dk, dv), q.dtype)
    _, o = lax.scan(step, s0, xs)
    return o.transpose(1, 0, 3, 2, 4).reshape(b, s, h, dv)


def mla_attention(qn, qr, kn, kr, v):
    b, s, h, _ = qn.shape
    nb = s // Q_BLOCK
    scale = (MLA_NOPE + MLA_ROPE) ** -0.5
    k_chunk = jnp.arange(s) // CHUNK

    def blocks(t):
        return jnp.moveaxis(t.reshape((b, nb, Q_BLOCK) + t.shape[2:]), 1, 0)

    def one(args):
        qn_b, qr_b, j = args
        sc = (jnp.einsum('bqhd,bkhd->bhqk', qn_b, kn)
              + jnp.einsum('bqhr,bkr->bhqk', qr_b, kr)).astype(jnp.float32) * scale
        q_chunk = (j * Q_BLOCK + jnp.arange(Q_BLOCK)) // CHUNK
        allowed = k_chunk[None, :] <= q_chunk[:, None]
        pr = jax.nn.softmax(jnp.where(allowed, sc, -jnp.inf), axis=-1)
        return jnp.einsum('bhqk,bkhd->bqhd', pr.astype(v.dtype), v)

    o = lax.map(one, (blocks(qn), blocks(qr), jnp.arange(nb)))
    return jnp.moveaxis(o, 0, 1).reshape(b, s, h * v.shape[-1])


def setup_inputs(seed: int = 0) -> dict:
    key = jax.random.key(seed)
    ks = iter(jax.random.split(key, 40))
    f32 = jnp.float32

    def nrm(shape, fan_in):
        return jax.random.normal(next(ks), shape, f32) * (fan_in ** -0.5)

    def gain(n):
        return 1.0 + 0.02 * jax.random.normal(next(ks), (DEPTH, n), f32)

    x = jax.random.normal(next(ks), (BATCH, SEQ, D_MODEL), f32)
    p = jax.random.normal(next(ks), (DEPTH, BATCH, SEQ, PLE_DIM), f32)
    offset = jax.random.randint(next(ks), (BATCH,), 0, 4096, dtype=jnp.int32)
    positions = offset[:, None] + jnp.arange(SEQ, dtype=jnp.int32)[None, :]
    dt = jnp.exp(jax.random.uniform(next(ks), (DEPTH, DN_HEADS), f32,
                                    minval=math.log(1e-3), maxval=math.log(0.1)))
    dt_bias = dt + jnp.log(-jnp.expm1(-dt))
    a_log = jnp.log(jax.random.uniform(next(ks), (DEPTH, DN_HEADS), f32, minval=1.0, maxval=16.0))
    return {
        "x": x,
        "p": p,
        "positions": positions,
        "mix_norm": gain(D_MODEL),
        "w_in": nrm((DEPTH, D_MODEL, D_IN), D_MODEL),
        "conv_w": nrm((DEPTH, CONV_K, 3 * DN_WIDTH), CONV_K),
        "dt_bias": dt_bias,
        "a_log": a_log,
        "dn_out_norm": gain(DN_HEAD_DIM),
        "ckv_norm": gain(KV_RANK),
        "w_kv_up": nrm((DEPTH, KV_RANK, MLA_HEADS * (MLA_NOPE + MLA_V)), KV_RANK),
        "q_nope_norm": gain(MLA_NOPE),
        "q_rope_norm": gain(MLA_ROPE),
        "k_nope_norm": gain(MLA_NOPE),
        "k_rope_norm": gain(MLA_ROPE),
        "w_branch_a": nrm((DEPTH, DN_WIDTH, D_MODEL), DN_WIDTH),
        "w_branch_b": nrm((DEPTH, MLA_WIDTH, D_MODEL), MLA_WIDTH),
        "w_out": nrm((DEPTH, D_MODEL, D_MODEL), D_MODEL),
        "mlp_norm": gain(D_MODEL),
        "w_mlp_up": nrm((DEPTH, D_MODEL, D_FF), D_MODEL),
        "w_mlp_down": nrm((DEPTH, D_FF, D_MODEL), D_FF),
        "ple_norm": gain(D_MODEL),
        "w_ple_gate": nrm((DEPTH, D_MODEL, D_MODEL), D_MODEL),
        "w_ple": nrm((DEPTH, PLE_DIM, D_MODEL), PLE_DIM),
    }


def reference(x, p, positions, mix_norm, w_in, conv_w, dt_bias, a_log, dn_out_norm, ckv_norm,
              w_kv_up, q_nope_norm, q_rope_norm, k_nope_norm, k_rope_norm, w_branch_a,
              w_branch_b, w_out, mlp_norm, w_mlp_up, w_mlp_down, ple_norm, w_ple_gate, w_ple):
    b, s, _ = x.shape
    split_points = [int(c) for c in np.cumsum(IN_SIZES)[:-1]]
    for i in range(DEPTH):
        h = rms_norm(x, mix_norm[i])
        proj = h @ w_in[i]
        (dn_q, dn_k, dn_v, dn_z, dn_b, dn_a, mla_q, mla_ckv, mla_kr,
         gate_a, gate_b) = jnp.split(proj, split_points, axis=-1)

        qkv = jax.nn.silu(causal_conv(jnp.concatenate([dn_q, dn_k, dn_v], axis=-1), conv_w[i]))
        cq, ck, cv = jnp.split(qkv, 3, axis=-1)
        hs = (b, s, DN_HEADS, DN_HEAD_DIM)
        q_a = l2norm(cq.reshape(hs))
        k_a = l2norm(ck.reshape(hs))
        v_a = cv.reshape(hs).astype(jnp.float32)
        beta = jax.nn.sigmoid(dn_b.astype(jnp.float32))
        g = -jnp.exp(a_log[i].astype(jnp.float32)) * jax.nn.softplus(
            dn_a.astype(jnp.float32) + dt_bias[i].astype(jnp.float32))
        o_a = gated_delta_rule(q_a, k_a, v_a, g, beta).astype(x.dtype)
        o_a = (rms_norm(o_a, dn_out_norm[i]) * jax.nn.silu(dn_z.reshape(hs))).reshape(b, s, DN_WIDTH)

        mq = mla_q.reshape(b, s, MLA_HEADS, MLA_NOPE + MLA_ROPE)
        qn = rms_norm(mq[..., :MLA_NOPE], q_nope_norm[i])
        qr = rope(rms_norm(mq[..., MLA_NOPE:], q_rope_norm[i]), positions)
        kv = (rms_norm(mla_ckv, ckv_norm[i]) @ w_kv_up[i]).reshape(b, s, MLA_HEADS, MLA_NOPE + MLA_V)
        kn = rms_norm(kv[..., :MLA_NOPE], k_nope_norm[i])
        v_b = kv[..., MLA_NOPE:]
        kr = rope(rms_norm(mla_kr, k_rope_norm[i]), positions)
        o_b = mla_attention(qn, qr, kn, kr, v_b)

        y = (jax.nn.sigmoid(gate_a) * (o_a @ w_branch_a[i])
             + jax.nn.sigmoid(gate_b) * (o_b @ w_branch_b[i]))
        x = x + y @ w_out[i]

        hm = rms_norm(x, mlp_norm[i])
        x = x + jnp.square(jax.nn.relu(hm @ w_mlp_up[i])) @ w_mlp_down[i]

        ple_gate = jax.nn.sigmoid(rms_norm(x, ple_norm[i]) @ w_ple_gate[i])
        x = x + ple_gate * (p[i] @ w_ple[i])
    return x
```

```python
import functools
import math

import jax
import jax.numpy as jnp
from jax import lax
from jax.experimental import pallas as pl
from jax.experimental.pallas import tpu as pltpu

F32 = jnp.float32
BF16 = jnp.bfloat16

D_MODEL = 2048
CHUNK = 64
PLE_DIM = 256
EPS = 1e-6
DN_HEADS = 8
DN_HEAD_DIM = 128
DN_WIDTH = DN_HEADS * DN_HEAD_DIM
CONV_K = 4
MLA_HEADS = 8
MLA_NOPE = 128
MLA_ROPE = 64
MLA_QK = MLA_NOPE + MLA_ROPE
MLA_V = 128
KV_RANK = 512
MLA_WIDTH = MLA_HEADS * MLA_V
ROPE_BASE = 10000.0
D_FF = 4 * D_MODEL

_OFF_Q = 0
_OFF_K = _OFF_Q + DN_WIDTH
_OFF_V = _OFF_K + DN_WIDTH
_OFF_Z = _OFF_V + DN_WIDTH
_OFF_BETA = _OFF_Z + DN_WIDTH
_OFF_ALPHA = _OFF_BETA + DN_HEADS
_OFF_MQ = _OFF_ALPHA + DN_HEADS
_OFF_CKV = _OFF_MQ + MLA_HEADS * MLA_QK
_OFF_KR = _OFF_CKV + KV_RANK
_OFF_GA = _OFF_KR + MLA_ROPE
_OFF_GB = _OFF_GA + D_MODEL
D_IN = _OFF_GB + D_MODEL

MQ_W = MLA_HEADS * MLA_QK
CKV_BLK = MQ_W // KV_RANK
DN_BLK0 = (MQ_W + KV_RANK) // DN_HEAD_DIM
GATE_BLK0 = (MQ_W + KV_RANK + 4 * DN_WIDTH) // D_MODEL
N_MAIN = MQ_W + KV_RANK + 4 * DN_WIDTH + 2 * D_MODEL
LANES = 128
TAIL_BETA = MLA_ROPE
TAIL_ALPHA = MLA_ROPE + DN_HEADS

VMEM_LIMIT = 56 * 1024 * 1024


def _cparams(*sem):
    return pltpu.CompilerParams(dimension_semantics=sem, vmem_limit_bytes=VMEM_LIMIT)


def _rms(t, g):
    return t * lax.rsqrt(jnp.mean(t * t, axis=-1, keepdims=True) + EPS) * g


def _dot(a, b):
    return jnp.dot(a, b, preferred_element_type=F32)


def _dot_nt(a, b):
    return lax.dot_general(a, b, (((1,), (1,)), ((), ())), preferred_element_type=F32)


def _dot_tn(a, b):
    return lax.dot_general(a, b, (((0,), (0,)), ((), ())), preferred_element_type=F32)


def _dot_f32(a, b):
    return jnp.dot(a, b, preferred_element_type=F32, precision=lax.Precision.HIGHEST)


def _resident(shape):
    nd = len(shape)
    return pl.BlockSpec(shape, lambda *_: (0,) * nd, pipeline_mode=pl.Buffered(1))


def _inproj_kernel(x_ref, g_ref, w_ref, wt_ref, o_ref, t_ref, h_ref):
    @pl.when(pl.program_id(1) == 0)
    def _():
        h = _rms(x_ref[...], g_ref[...]).astype(BF16)
        h_ref[...] = h
        t_ref[...] = _dot(h, wt_ref[...])

    o_ref[...] = _dot(h_ref[...], w_ref[...]).astype(o_ref.dtype)


def _in_proj(x2, gain, w_main, w_tail):
    t = x2.shape[0]
    tm = min(1024, t)
    tn = 1024
    return pl.pallas_call(
        _inproj_kernel,
        grid=(t // tm, N_MAIN // tn),
        in_specs=[
            pl.BlockSpec((tm, D_MODEL), lambda i, j: (i, 0)),
            _resident((1, D_MODEL)),
            pl.BlockSpec((D_MODEL, tn), lambda i, j: (0, j)),
            _resident((D_MODEL, LANES)),
        ],
        out_specs=[
            pl.BlockSpec((tm, tn), lambda i, j: (i, j)),
            pl.BlockSpec((tm, LANES), lambda i, j: (i, 0)),
        ],
        out_shape=[jax.ShapeDtypeStruct((t, N_MAIN), BF16), jax.ShapeDtypeStruct((t, LANES), F32)],
        scratch_shapes=[pltpu.VMEM((tm, D_MODEL), BF16)],
        compiler_params=_cparams("parallel", "arbitrary"),
        name="in_proj",
    )(x2, gain, w_main, w_tail)


def _gdn_kernel(alog_ref, dtb_ref, q_ref, k_ref, v_ref, z_ref, t_ref, cwq_ref, cwk_ref, cwv_ref, on_ref,
                o_ref,
                g_s, qs_s, qd_s, k_s, kb_s, kbe_s, kd_s, vb_s, w_s, u_s, qk_s, oa_s):
    s = q_ref.shape[0]
    n_chunks = s // CHUNK
    head = pl.program_id(1)
    rows = lax.broadcasted_iota(jnp.int32, (s, 1), 0)

    def conv_silu(u_ref, w_ref):
        u = u_ref[...].astype(F32)
        w = w_ref[...]
        acc = u * w[CONV_K - 1:CONV_K, :]
        for d in range(1, CONV_K):
            shifted = jnp.where(rows >= d, pltpu.roll(u, d, 0), 0.0)
            acc = acc + shifted * w[CONV_K - 1 - d:CONV_K - d, :]
        return acc * jax.nn.sigmoid(acc)

    def l2norm(t):
        return t * lax.rsqrt(jnp.sum(t * t, axis=-1, keepdims=True) + EPS)

    qn = l2norm(conv_silu(q_ref, cwq_ref))
    kn = l2norm(conv_silu(k_ref, cwk_ref))
    vv = conv_silu(v_ref, cwv_ref)

    tail = t_ref[...]
    lane = lax.broadcasted_iota(jnp.int32, (1, LANES), 1)
    beta = jax.nn.sigmoid(jnp.sum(jnp.where(lane == TAIL_BETA + head, tail, 0.0), axis=1, keepdims=True))
    alpha = jnp.sum(jnp.where(lane == TAIL_ALPHA + head, tail, 0.0), axis=1, keepdims=True)
    a_coef = -jnp.exp(jnp.full((1, 1), alog_ref[head], F32))
    xs = alpha + dtb_ref[head]
    softplus = jnp.maximum(xs, 0.0) + jnp.log1p(jnp.exp(-jnp.abs(xs)))
    g = a_coef * softplus

    gc = jnp.broadcast_to(g, (s, LANES))
    pos = rows % CHUNK
    step = 1
    while step < CHUNK:
        gc = gc + jnp.where(pos >= step, pltpu.roll(gc, step, 0), 0.0)
        step *= 2
    gc3 = gc.reshape(n_chunks, CHUNK, LANES)
    g_last = jnp.broadcast_to(gc3[:, CHUNK - 1:CHUNK, :], gc3.shape).reshape(s, LANES)
    eg = jnp.exp(gc)

    qs = qn * (DN_HEAD_DIM ** -0.5)
    kb = kn * beta
    g_s[...] = gc
    qs_s[...] = qs.astype(BF16)
    qd_s[...] = (qs * eg).astype(BF16)
    k_s[...] = kn.astype(BF16)
    kb_s[...] = kb.astype(BF16)
    kbe_s[...] = (kb * eg).astype(BF16)
    kd_s[...] = (kn * jnp.exp(g_last - gc)).astype(BF16)
    vb_s[...] = (vv * beta).astype(BF16)

    ri = lax.broadcasted_iota(jnp.int32, (CHUNK, CHUNK), 0)
    ci = lax.broadcasted_iota(jnp.int32, (CHUNK, CHUNK), 1)
    eye = ri == ci
    eye_f = jnp.where(eye, 1.0, 0.0).astype(F32)

    def prep(n, carry):
        sl = pl.ds(pl.multiple_of(n * CHUNK, CHUNK), CHUNK)
        g_col = g_s[sl, :][:, :CHUNK]
        g_row = jnp.sum(jnp.where(eye, g_col, 0.0), axis=0, keepdims=True)
        decay = jnp.exp(jnp.where(ri >= ci, g_col - g_row, -jnp.inf))
        kc = k_s[sl, :]
        a = jnp.where(ri > ci, _dot_nt(kb_s[sl, :], kc) * decay, 0.0)
        qk_s[sl, :] = (_dot_nt(qs_s[sl, :], kc) * decay).astype(BF16)
        t_inv = eye_f - a
        m = a
        width = 2
        while width < CHUNK:
            m = _dot_f32(m, m)
            t_inv = t_inv + _dot_f32(t_inv, m)
            width *= 2
        t_b = t_inv.astype(BF16)
        w_s[sl, :] = _dot(t_b, kbe_s[sl, :]).astype(BF16)
        u_s[sl, :] = _dot(t_b, vb_s[sl, :])
        return carry

    lax.fori_loop(0, n_chunks, prep, 0)

    def scan(n, state):
        r0 = pl.multiple_of(n * CHUNK, CHUNK)
        sl = pl.ds(r0, CHUNK)
        st_b = state.astype(BF16)
        v_new = u_s[sl, :] - _dot(w_s[sl, :], st_b)
        v_nb = v_new.astype(BF16)
        oa_s[sl, :] = _dot(qd_s[sl, :], st_b) + _dot(qk_s[sl, :], v_nb)
        decay_last = jnp.exp(g_s[pl.ds(r0 + CHUNK - 1, 1), :])
        return state * decay_last + _dot_tn(kd_s[sl, :], v_nb)

    lax.fori_loop(0, n_chunks, scan, jnp.zeros((DN_HEAD_DIM, DN_HEAD_DIM), F32))

    z = z_ref[...].astype(F32)
    o_ref[...] = (_rms(oa_s[...], on_ref[...]) * (z * jax.nn.sigmoid(z))).astype(o_ref.dtype)


def _gdn(proj, tail, conv_w, a_log, dt_bias, out_norm, b, s):
    hd = DN_HEAD_DIM
    nh = DN_HEADS
    col = lambda blk0: pl.BlockSpec((s, hd), lambda bi, h: (bi, blk0 + h))
    cw = lambda blk0: pl.BlockSpec((CONV_K, hd), lambda bi, h: (0, blk0 + h))
    smem = pl.BlockSpec(memory_space=pltpu.SMEM)
    big = lambda dt: pltpu.VMEM((s, hd), dt)
    return pl.pallas_call(
        _gdn_kernel,
        grid=(b, nh),
        in_specs=[
            smem, smem,
            col(DN_BLK0), col(DN_BLK0 + nh), col(DN_BLK0 + 2 * nh), col(DN_BLK0 + 3 * nh),
            pl.BlockSpec((s, LANES), lambda bi, h: (bi, 0)),
            cw(0), cw(nh), cw(2 * nh),
            _resident((1, hd)),
        ],
        out_specs=pl.BlockSpec((s, hd), lambda bi, h: (bi, h)),
        out_shape=jax.ShapeDtypeStruct((b * s, DN_WIDTH), BF16),
        scratch_shapes=[
            big(F32),
            big(BF16), big(BF16), big(BF16), big(BF16), big(BF16), big(BF16), big(BF16),
            big(BF16),
            big(F32),
            pltpu.VMEM((s, CHUNK), BF16),
            big(F32),
        ],
        compiler_params=_cparams("parallel", "arbitrary"),
        name="gdn",
    )(a_log, dt_bias, proj, proj, proj, proj, tail, conv_w, conv_w, conv_w, out_norm)


def _mla_prep_kernel(mq_ref, ckv_ref, t_ref, pos_ref, inv_ref, qnn_ref, qrn_ref, cn_ref, knn_ref, krn_ref, wkv_ref,
                     q_out, k_out, v_out):
    half = MLA_ROPE // 2
    ang = pos_ref[...].astype(F32) * inv_ref[...]
    cos = jnp.cos(ang)
    sin = jnp.sin(ang)

    def rope(t):
        rot = jnp.concatenate([-t[:, half:], t[:, :half]], axis=1)
        return t * cos + rot * sin

    scale = MLA_QK ** -0.5
    kr = rope(_rms(t_ref[...][:, :MLA_ROPE], krn_ref[...])).astype(BF16)
    c = _rms(ckv_ref[...].astype(F32), cn_ref[...]).astype(BF16)
    kv = _dot(c, wkv_ref[...])
    mq = mq_ref[...].astype(F32)
    rope0 = MLA_HEADS * MLA_NOPE
    for h in range(MLA_HEADS):
        qn = _rms(mq[:, h * MLA_NOPE:(h + 1) * MLA_NOPE], qnn_ref[...]) * scale
        qr = rope(_rms(mq[:, rope0 + h * MLA_ROPE:rope0 + (h + 1) * MLA_ROPE], qrn_ref[...])) * scale
        q_out[0, h, :, :MLA_NOPE] = qn.astype(BF16)
        q_out[0, h, :, MLA_NOPE:] = qr.astype(BF16)
        kv0 = h * (MLA_NOPE + MLA_V)
        k_out[0, h, :, :MLA_NOPE] = _rms(kv[:, kv0:kv0 + MLA_NOPE], knn_ref[...]).astype(BF16)
        k_out[0, h, :, MLA_NOPE:] = kr
        v_out[0, h, :, :] = kv[:, kv0 + MLA_NOPE:kv0 + MLA_NOPE + MLA_V].astype(BF16)


def _mla_prep(proj, tail, pos, inv64, qnn, qrn, cn, knn, krn, wkv, b, s):
    tm = min(512, s)
    nsb = s // tm
    nh = MLA_HEADS
    row = lambda w, blk: pl.BlockSpec((tm, w), lambda i: (i, blk))
    outspec = lambda w: pl.BlockSpec((1, nh, tm, w), lambda i: (i // nsb, 0, i % nsb, 0))
    return pl.pallas_call(
        _mla_prep_kernel,
        grid=(b * nsb,),
        in_specs=[
            row(MQ_W, 0), row(KV_RANK, CKV_BLK), row(LANES, 0), row(1, 0),
            _resident((1, MLA_ROPE)), _resident((1, MLA_NOPE)), _resident((1, MLA_ROPE)),
            _resident((1, KV_RANK)), _resident((1, MLA_NOPE)), _resident((1, MLA_ROPE)),
            _resident((KV_RANK, nh * (MLA_NOPE + MLA_V))),
        ],
        out_specs=[outspec(MLA_QK), outspec(MLA_QK), outspec(MLA_V)],
        out_shape=[
            jax.ShapeDtypeStruct((b, nh, s, MLA_QK), BF16),
            jax.ShapeDtypeStruct((b, nh, s, MLA_QK), BF16),
            jax.ShapeDtypeStruct((b, nh, s, MLA_V), BF16),
        ],
        compiler_params=_cparams("parallel"),
        name="mla_prep",
    )(proj, proj, tail, pos, inv64, qnn, qrn, cn, knn, krn, wkv)


def _attn_kernel(q_ref, k_ref, v_ref, o_ref, *, tq):
    s = q_ref.shape[2]
    ri = lax.broadcasted_iota(jnp.int32, (tq, tq), 0) // CHUNK
    ci = lax.broadcasted_iota(jnp.int32, (tq, tq), 1) // CHUNK
    allowed = ci <= ri
    for qi in range(s // tq):
        lo, hi = qi * tq, (qi + 1) * tq
        q = q_ref[0, 0, lo:hi, :]
        s_d = jnp.where(allowed, _dot_nt(q, k_ref[0, 0, lo:hi, :]), -jnp.inf)
        m = jnp.max(s_d, axis=-1, keepdims=True)
        if qi > 0:
            s_o = _dot_nt(q, k_ref[0, 0, :lo, :])
            m = jnp.maximum(m, jnp.max(s_o, axis=-1, keepdims=True))
            p_o = jnp.exp(s_o - m)
            den = jnp.sum(p_o, axis=-1, keepdims=True)
            acc = _dot(p_o.astype(BF16), v_ref[0, 0, :lo, :])
        p_d = jnp.exp(s_d - m)
        if qi > 0:
            den = den + jnp.sum(p_d, axis=-1, keepdims=True)
            acc = acc + _dot(p_d.astype(BF16), v_ref[0, 0, lo:hi, :])
        else:
            den = jnp.sum(p_d, axis=-1, keepdims=True)
            acc = _dot(p_d.astype(BF16), v_ref[0, 0, lo:hi, :])
        o_ref[lo:hi, :] = (acc / den).astype(o_ref.dtype)


def _mla_attn(q, k, v):
    b, nh, s, _ = q.shape
    tq = min(256, s)
    spec = lambda w: pl.BlockSpec((1, 1, s, w), lambda bi, h: (bi, h, 0, 0))
    return pl.pallas_call(
        functools.partial(_attn_kernel, tq=tq),
        grid=(b, nh),
        in_specs=[spec(MLA_QK), spec(MLA_QK), spec(MLA_V)],
        out_specs=pl.BlockSpec((s, MLA_V), lambda bi, h: (bi, h)),
        out_shape=jax.ShapeDtypeStruct((b * s, MLA_WIDTH), BF16),
        compiler_params=_cparams("parallel", "parallel"),
        name="mla_attn",
    )(q, k, v)


def _merge_kernel(x_ref, oa_ref, ob_ref, ga_ref, gb_ref, wa_ref, wb_ref, wo_ref, o_ref):
    ya = jax.nn.sigmoid(ga_ref[...].astype(F32)) * _dot(oa_ref[...], wa_ref[...])
    yb = jax.nn.sigmoid(gb_ref[...].astype(F32)) * _dot(ob_ref[...], wb_ref[...])
    o_ref[...] = x_ref[...] + _dot((ya + yb).astype(BF16), wo_ref[...])


def _merge(x2, o_a, o_b, proj, w_a, w_b, w_o):
    t = x2.shape[0]
    tm = min(512, t)
    row = lambda w, blk: pl.BlockSpec((tm, w), lambda i: (i, blk))
    return pl.pallas_call(
        _merge_kernel,
        grid=(t // tm,),
        in_specs=[
            row(D_MODEL, 0), row(DN_WIDTH, 0), row(MLA_WIDTH, 0), row(D_MODEL, GATE_BLK0), row(D_MODEL, GATE_BLK0 + 1),
            _resident((DN_WIDTH, D_MODEL)), _resident((MLA_WIDTH, D_MODEL)), _resident((D_MODEL, D_MODEL)),
        ],
        out_specs=row(D_MODEL, 0),
        out_shape=jax.ShapeDtypeStruct((t, D_MODEL), F32),
        compiler_params=_cparams("parallel"),
        name="merge",
    )(x2, o_a, o_b, proj, proj, w_a, w_b, w_o)


def _mlp_kernel(x_ref, gm_ref, up_ref, dn_ref, gp_ref, wg_ref, p_ref, wp_ref, o_ref, hm_ref):
    f = pl.program_id(1)

    @pl.when(f == 0)
    def _():
        x = x_ref[...]
        hm_ref[...] = _rms(x, gm_ref[...]).astype(BF16)
        o_ref[...] = x

    a = _dot(hm_ref[...], up_ref[...])
    o_ref[...] += _dot(jnp.square(jnp.maximum(a, 0.0)).astype(BF16), dn_ref[...])

    @pl.when(f == pl.num_programs(1) - 1)
    def _():
        x2 = o_ref[...]
        gate = jax.nn.sigmoid(_dot(_rms(x2, gp_ref[...]).astype(BF16), wg_ref[...]))
        o_ref[...] = x2 + gate * _dot(p_ref[...].astype(BF16), wp_ref[...])


def _mlp_ple(x1, gm, w_up, w_dn, gp, w_gate, p2, w_ple):
    t = x1.shape[0]
    tm = min(512, t)
    tf = 1024
    return pl.pallas_call(
        _mlp_kernel,
        grid=(t // tm, D_FF // tf),
        in_specs=[
            pl.BlockSpec((tm, D_MODEL), lambda i, f: (i, 0)),
            _resident((1, D_MODEL)),
            pl.BlockSpec((D_MODEL, tf), lambda i, f: (0, f)),
            pl.BlockSpec((tf, D_MODEL), lambda i, f: (f, 0)),
            _resident((1, D_MODEL)),
            _resident((D_MODEL, D_MODEL)),
            pl.BlockSpec((tm, PLE_DIM), lambda i, f: (i, 0)),
            _resident((PLE_DIM, D_MODEL)),
        ],
        out_specs=pl.BlockSpec((tm, D_MODEL), lambda i, f: (i, 0)),
        out_shape=jax.ShapeDtypeStruct((t, D_MODEL), F32),
        scratch_shapes=[pltpu.VMEM((tm, D_MODEL), BF16)],
        compiler_params=_cparams("parallel", "arbitrary"),
        name="mlp_ple",
    )(x1, gm, w_up, w_dn, gp, w_gate, p2, w_ple)


def _regroup_w_in(w):
    d = w.shape[0]
    mq = w[:, _OFF_MQ:_OFF_CKV].reshape(d, MLA_HEADS, MLA_QK)
    main = jnp.concatenate([
        mq[:, :, :MLA_NOPE].reshape(d, MLA_HEADS * MLA_NOPE),
        mq[:, :, MLA_NOPE:].reshape(d, MLA_HEADS * MLA_ROPE),
        w[:, _OFF_CKV:_OFF_KR],
        w[:, _OFF_Q:_OFF_BETA],
        w[:, _OFF_GA:D_IN],
    ], axis=1).astype(BF16)
    tail = jnp.concatenate([
        w[:, _OFF_KR:_OFF_GA],
        w[:, _OFF_BETA:_OFF_MQ],
        jnp.zeros((d, LANES - MLA_ROPE - 2 * DN_HEADS), w.dtype),
    ], axis=1).astype(BF16)
    return main, tail


def kernel(x, p, positions, mix_norm, w_in, conv_w, dt_bias, a_log, dn_out_norm, ckv_norm, w_kv_up, q_nope_norm,
           q_rope_norm, k_nope_norm, k_rope_norm, w_branch_a, w_branch_b, w_out, mlp_norm, w_mlp_up, w_mlp_down,
           ple_norm, w_ple_gate, w_ple):
    b, s, d = x.shape
    t = b * s
    depth = w_in.shape[0]
    half = MLA_ROPE // 2
    inv = ROPE_BASE ** (-jnp.arange(half, dtype=F32) / half)
    inv64 = jnp.concatenate([inv, inv]).reshape(1, MLA_ROPE)
    pos = positions.reshape(t, 1)
    row = lambda a: a.reshape(1, -1).astype(F32)

    x2 = x.reshape(t, d)
    for i in range(depth):
        w_main, w_tail = _regroup_w_in(w_in[i])
        proj, tail = _in_proj(x2, row(mix_norm[i]), w_main, w_tail)
        o_a = _gdn(proj, tail, conv_w[i], a_log[i], dt_bias[i], row(dn_out_norm[i]), b, s)
        q_b, k_b, v_b = _mla_prep(proj, tail, pos, inv64, row(q_nope_norm[i]), row(q_rope_norm[i]),
                                  row(ckv_norm[i]), row(k_nope_norm[i]), row(k_rope_norm[i]),
                                  w_kv_up[i].astype(BF16), b, s)
        o_b = _mla_attn(q_b, k_b, v_b)
        x1 = _merge(x2, o_a, o_b, proj, w_branch_a[i].astype(BF16), w_branch_b[i].astype(BF16),
                    w_out[i].astype(BF16))
        x2 = _mlp_ple(x1, row(mlp_norm[i]), w_mlp_up[i].astype(BF16), w_mlp_down[i].astype(BF16),
                      row(ple_norm[i]), w_ple_gate[i].astype(BF16), p[i].reshape(t, PLE_DIM), w_ple[i].astype(BF16))
    return x2.reshape(b, s, d)
```

```python
import functools

import jax
import jax.numpy as jnp
from jax import lax
from jax.experimental import pallas as pl
from jax.experimental.pallas import tpu as pltpu

F32 = jnp.float32
BF16 = jnp.bfloat16

D_MODEL = 2048
CHUNK = 64
PLE_DIM = 256
EPS = 1e-6
DN_HEADS = 8
DN_HEAD_DIM = 128
DN_WIDTH = DN_HEADS * DN_HEAD_DIM
CONV_K = 4
MLA_HEADS = 8
MLA_NOPE = 128
MLA_ROPE = 64
MLA_QK = MLA_NOPE + MLA_ROPE
MLA_V = 128
KV_RANK = 512
MLA_WIDTH = MLA_HEADS * MLA_V
ROPE_BASE = 10000.0
D_FF = 4 * D_MODEL

_OFF_Q = 0
_OFF_K = _OFF_Q + DN_WIDTH
_OFF_V = _OFF_K + DN_WIDTH
_OFF_Z = _OFF_V + DN_WIDTH
_OFF_BETA = _OFF_Z + DN_WIDTH
_OFF_ALPHA = _OFF_BETA + DN_HEADS
_OFF_MQ = _OFF_ALPHA + DN_HEADS
_OFF_CKV = _OFF_MQ + MLA_HEADS * MLA_QK
_OFF_KR = _OFF_CKV + KV_RANK
_OFF_GA = _OFF_KR + MLA_ROPE
_OFF_GB = _OFF_GA + D_MODEL
D_IN = _OFF_GB + D_MODEL

MQ_W = MLA_HEADS * MLA_QK
CKV_BLK = MQ_W // KV_RANK
GATE_BLK0 = (MQ_W + KV_RANK + 4 * DN_WIDTH) // D_MODEL
N_MAIN = MQ_W + KV_RANK + 4 * DN_WIDTH + 2 * D_MODEL
LANES = 128
TAIL_BETA = MLA_ROPE
TAIL_ALPHA = MLA_ROPE + DN_HEADS

VMEM_LIMIT = 56 * 1024 * 1024


def _cparams(*sem):
    return pltpu.CompilerParams(dimension_semantics=sem, vmem_limit_bytes=VMEM_LIMIT)


def _rms(t, g):
    return t * lax.rsqrt(jnp.mean(t * t, axis=-1, keepdims=True) + EPS) * g


def _dot(a, b):
    return jnp.dot(a, b, preferred_element_type=F32)


def _dot_nt(a, b):
    return lax.dot_general(a, b, (((1,), (1,)), ((), ())), preferred_element_type=F32)


def _dot_tn(a, b):
    return lax.dot_general(a, b, (((0,), (0,)), ((), ())), preferred_element_type=F32)


def _resident(shape):
    nd = len(shape)
    return pl.BlockSpec(shape, lambda *_: (0,) * nd, pipeline_mode=pl.Buffered(1))


def _inproj_kernel(x_ref, g_ref, w_ref, wt_ref, o_ref, t_ref, h_ref):
    @pl.when(pl.program_id(1) == 0)
    def _():
        h = _rms(x_ref[...], g_ref[...]).astype(BF16)
        h_ref[...] = h
        t_ref[...] = _dot(h, wt_ref[...])

    o_ref[...] = _dot(h_ref[...], w_ref[...]).astype(o_ref.dtype)


def _in_proj(x2, gain, w_main, w_tail):
    t = x2.shape[0]
    tm = min(1024, t)
    tn = 1024
    return pl.pallas_call(
        _inproj_kernel,
        grid=(t // tm, N_MAIN // tn),
        in_specs=[
            pl.BlockSpec((tm, D_MODEL), lambda i, j: (i, 0)),
            _resident((1, D_MODEL)),
            pl.BlockSpec((D_MODEL, tn), lambda i, j: (0, j)),
            _resident((D_MODEL, LANES)),
        ],
        out_specs=[
            pl.BlockSpec((tm, tn), lambda i, j: (i, j)),
            pl.BlockSpec((tm, LANES), lambda i, j: (i, 0)),
        ],
        out_shape=[jax.ShapeDtypeStruct((t, N_MAIN), BF16), jax.ShapeDtypeStruct((t, LANES), F32)],
        scratch_shapes=[pltpu.VMEM((tm, D_MODEL), BF16)],
        compiler_params=_cparams("parallel", "arbitrary"),
        name="in_proj",
    )(x2, gain, w_main, w_tail)


GDN_GROUP = 4
GDN_PREP_CHUNKS = 4


def _dot_r(x, m):
    return _dot(x.astype(BF16), m.astype(BF16))


def _gdn_kernel(alog_ref, dtb_ref, q_ref, k_ref, v_ref, z_ref, t_ref, cwq_ref, cwk_ref, cwv_ref, on_ref,
                o_ref,
                g_s, k_s, kq_s, kv_s, wq_s, kd_s, u_s, qk_s, st_s):
    s = q_ref.shape[0]
    n_chunks = s // CHUNK
    hd = DN_HEAD_DIM
    group = pl.program_id(1)
    rows = lax.broadcasted_iota(jnp.int32, (s, 1), 0)
    pos = rows % CHUNK
    lane = lax.broadcasted_iota(jnp.int32, (1, LANES), 1)

    def conv_silu(u_ref, w_ref, c0):
        u = u_ref[:, c0:c0 + hd].astype(F32)
        w = w_ref[:, c0:c0 + hd]
        acc = u * w[CONV_K - 1:CONV_K, :]
        for d in range(1, CONV_K):
            shifted = jnp.where(rows >= d, pltpu.roll(u, d, 0), 0.0)
            acc = acc + shifted * w[CONV_K - 1 - d:CONV_K - d, :]
        return acc * jax.nn.sigmoid(acc)

    def l2norm(t):
        return t * lax.rsqrt(jnp.sum(t * t, axis=-1, keepdims=True) + EPS)

    def chunked(t):
        return t.reshape(n_chunks, CHUNK, hd)

    for hh in range(GDN_GROUP):
        head = group * GDN_GROUP + hh
        c0 = hh * hd
        qn = l2norm(conv_silu(q_ref, cwq_ref, c0))
        kn = l2norm(conv_silu(k_ref, cwk_ref, c0))
        vv = conv_silu(v_ref, cwv_ref, c0)

        tail = t_ref[...]
        beta = jax.nn.sigmoid(jnp.sum(jnp.where(lane == TAIL_BETA + head, tail, 0.0), axis=1, keepdims=True))
        alpha = jnp.sum(jnp.where(lane == TAIL_ALPHA + head, tail, 0.0), axis=1, keepdims=True)
        a_coef = -jnp.exp(jnp.full((1, 1), alog_ref[head], F32))
        xs = alpha + dtb_ref[head]
        softplus = jnp.maximum(xs, 0.0) + jnp.log1p(jnp.exp(-jnp.abs(xs)))
        g = a_coef * softplus

        gc = jnp.broadcast_to(g, (s, LANES))
        step = 1
        while step < CHUNK:
            gc = gc + jnp.where(pos >= step, pltpu.roll(gc, step, 0), 0.0)
            step *= 2
        gc3 = chunked(gc)
        g_last = jnp.broadcast_to(gc3[:, CHUNK - 1:CHUNK, :], gc3.shape).reshape(s, LANES)
        eg = jnp.exp(gc)

        qs = qn * (hd ** -0.5)
        kb = kn * beta
        g_s[hh] = gc
        k_s[hh] = kn.astype(BF16)
        kq_s[hh, :, :CHUNK, :] = chunked(kb.astype(BF16))
        kq_s[hh, :, CHUNK:, :] = chunked(qs.astype(BF16))
        kv_s[hh, :, :hd] = (kb * eg).astype(BF16)
        kv_s[hh, :, hd:] = (vv * beta).astype(BF16)
        wq_s[hh, :, CHUNK:, :] = chunked((qs * eg).astype(BF16))
        kd_s[hh] = (kn * jnp.exp(g_last - gc)).astype(BF16)
        st_s[hh] = jnp.zeros((hd, hd), F32)

    ri = lax.broadcasted_iota(jnp.int32, (CHUNK, CHUNK), 0)
    ci = lax.broadcasted_iota(jnp.int32, (CHUNK, CHUNK), 1)
    eye = ri == ci
    eye_f = jnp.where(eye, 1.0, 0.0).astype(F32)
    n_factors = CHUNK.bit_length() - 2

    heads = range(GDN_GROUP)

    def prep(it, carry):
        items = [(hh, it * GDN_PREP_CHUNKS + c) for c in range(GDN_PREP_CHUNKS) for hh in heads]
        sls = [pl.ds(pl.multiple_of(n * CHUNK, CHUNK), CHUNK) for _, n in items]
        rs = [_dot_nt(kq_s[hh, n], k_s[hh, sl, :]) for (hh, n), sl in zip(items, sls)]
        ms, ps = [], []
        for (hh, n), sl, r in zip(items, sls, rs):
            g_col = g_s[hh, sl, :][:, :CHUNK]
            g_row = jnp.sum(jnp.where(eye, g_col, 0.0), axis=0, keepdims=True)
            decay = jnp.exp(jnp.where(ri >= ci, g_col - g_row, -jnp.inf))
            a = jnp.where(ri > ci, r[:CHUNK] * decay, 0.0)
            qk_s[hh, sl, :] = (r[CHUNK:] * decay).astype(BF16)
            ms.append(a)
            ps.append(eye_f - a)
        ms = [_dot_r(a, a) for a in ms]
        for _ in range(n_factors - 1):
            rs = [_dot_r(jnp.concatenate([m, p], axis=0), m) for m, p in zip(ms, ps)]
            ms = [r[:CHUNK] for r in rs]
            ps = [p + r[CHUNK:] for p, r in zip(ps, rs)]
        rs = [_dot_r(p, m) for m, p in zip(ms, ps)]
        wus = [_dot((p + r).astype(BF16), kv_s[hh, sl, :]) for (hh, _), sl, p, r in zip(items, sls, ps, rs)]
        for (hh, n), sl, wu in zip(items, sls, wus):
            wq_s[hh, n, :CHUNK, :] = wu[:, :hd].astype(BF16)
            u_s[hh, sl, :] = wu[:, hd:]
        return carry

    lax.fori_loop(0, n_chunks // GDN_PREP_CHUNKS, prep, 0)

    def scan(n, carry):
        r0 = pl.multiple_of(n * CHUNK, CHUNK)
        sl = pl.ds(r0, CHUNK)
        states = [st_s[hh] for hh in heads]
        rs = [_dot(wq_s[hh, n], states[hh].astype(BF16)) for hh in heads]
        v_nbs = [(u_s[hh, sl, :] - rs[hh][:CHUNK]).astype(BF16) for hh in heads]
        os = [rs[hh][CHUNK:] + _dot(qk_s[hh, sl, :], v_nbs[hh]) for hh in heads]
        kvs = [_dot_tn(kd_s[hh, sl, :], v_nbs[hh]) for hh in heads]
        for hh in heads:
            decay_last = jnp.exp(g_s[hh, pl.ds(r0 + CHUNK - 1, 1), :])
            st_s[hh] = states[hh] * decay_last + kvs[hh]
            z = z_ref[sl, hh * hd:(hh + 1) * hd].astype(F32)
            o_ref[sl, hh * hd:(hh + 1) * hd] = (_rms(os[hh], on_ref[...]) * (z * jax.nn.sigmoid(z))).astype(o_ref.dtype)
        return carry

    lax.fori_loop(0, n_chunks, scan, 0)


def _gdn(proj, tail, conv_w, a_log, dt_bias, out_norm, b, s):
    hd = DN_HEAD_DIM
    gw = hd * GDN_GROUP
    n_groups = DN_HEADS // GDN_GROUP
    n_chunks = s // CHUNK
    blk0 = (MQ_W + KV_RANK) // gw
    col = lambda k: pl.BlockSpec((s, gw), lambda bi, g: (bi, blk0 + k * n_groups + g))
    cw = lambda k: pl.BlockSpec((CONV_K, gw), lambda bi, g: (0, k * n_groups + g))
    smem = pl.BlockSpec(memory_space=pltpu.SMEM)
    per_head = lambda shape, dt: pltpu.VMEM((GDN_GROUP,) + shape, dt)
    return pl.pallas_call(
        _gdn_kernel,
        grid=(b, n_groups),
        in_specs=[
            smem, smem,
            col(0), col(1), col(2), col(3),
            pl.BlockSpec((s, LANES), lambda bi, g: (bi, 0)),
            cw(0), cw(1), cw(2),
            _resident((1, hd)),
        ],
        out_specs=pl.BlockSpec((s, gw), lambda bi, g: (bi, g)),
        out_shape=jax.ShapeDtypeStruct((b * s, DN_WIDTH), BF16),
        scratch_shapes=[
            per_head((s, LANES), F32),
            per_head((s, hd), BF16),
            per_head((n_chunks, 2 * CHUNK, hd), BF16),
            per_head((s, 2 * hd), BF16),
            per_head((n_chunks, 2 * CHUNK, hd), BF16),
            per_head((s, hd), BF16),
            per_head((s, hd), F32),
            per_head((s, CHUNK), BF16),
            per_head((hd, hd), F32),
        ],
        compiler_params=_cparams("parallel", "arbitrary"),
        name="gdn",
    )(a_log, dt_bias, proj, proj, proj, proj, tail, conv_w, conv_w, conv_w, out_norm)


def _mla_prep_kernel(mq_ref, ckv_ref, t_ref, pos_ref, inv_ref, qnn_ref, qrn_ref, cn_ref, knn_ref, krn_ref, wkv_ref,
                     q_out, k_out, v_out):
    half = MLA_ROPE // 2
    ang = pos_ref[...].astype(F32) * inv_ref[...]
    cos = jnp.cos(ang)
    sin = jnp.sin(ang)

    def rope(t):
        rot = jnp.concatenate([-t[:, half:], t[:, :half]], axis=1)
        return t * cos + rot * sin

    scale = MLA_QK ** -0.5
    kr = rope(_rms(t_ref[...][:, :MLA_ROPE], krn_ref[...])).astype(BF16)
    c = _rms(ckv_ref[...].astype(F32), cn_ref[...]).astype(BF16)
    kv = _dot(c, wkv_ref[...])
    mq = mq_ref[...].astype(F32)
    rope0 = MLA_HEADS * MLA_NOPE
    for h in range(MLA_HEADS):
        qn = _rms(mq[:, h * MLA_NOPE:(h + 1) * MLA_NOPE], qnn_ref[...]) * scale
        qr = rope(_rms(mq[:, rope0 + h * MLA_ROPE:rope0 + (h + 1) * MLA_ROPE], qrn_ref[...])) * scale
        q_out[0, h, :, :MLA_NOPE] = qn.astype(BF16)
        q_out[0, h, :, MLA_NOPE:] = qr.astype(BF16)
        kv0 = h * (MLA_NOPE + MLA_V)
        k_out[0, h, :, :MLA_NOPE] = _rms(kv[:, kv0:kv0 + MLA_NOPE], knn_ref[...]).astype(BF16)
        k_out[0, h, :, MLA_NOPE:] = kr
        v_out[0, h, :, :] = kv[:, kv0 + MLA_NOPE:kv0 + MLA_NOPE + MLA_V].astype(BF16)


def _mla_prep(proj, tail, pos, inv64, qnn, qrn, cn, knn, krn, wkv, b, s):
    tm = min(512, s)
    nsb = s // tm
    nh = MLA_HEADS
    row = lambda w, blk: pl.BlockSpec((tm, w), lambda i: (i, blk))
    outspec = lambda w: pl.BlockSpec((1, nh, tm, w), lambda i: (i // nsb, 0, i % nsb, 0))
    return pl.pallas_call(
        _mla_prep_kernel,
        grid=(b * nsb,),
        in_specs=[
            row(MQ_W, 0), row(KV_RANK, CKV_BLK), row(LANES, 0), row(1, 0),
            _resident((1, MLA_ROPE)), _resident((1, MLA_NOPE)), _resident((1, MLA_ROPE)),
            _resident((1, KV_RANK)), _resident((1, MLA_NOPE)), _resident((1, MLA_ROPE)),
            _resident((KV_RANK, nh * (MLA_NOPE + MLA_V))),
        ],
        out_specs=[outspec(MLA_QK), outspec(MLA_QK), outspec(MLA_V)],
        out_shape=[
            jax.ShapeDtypeStruct((b, nh, s, MLA_QK), BF16),
            jax.ShapeDtypeStruct((b, nh, s, MLA_QK), BF16),
            jax.ShapeDtypeStruct((b, nh, s, MLA_V), BF16),
        ],
        compiler_params=_cparams("parallel"),
        name="mla_prep",
    )(proj, proj, tail, pos, inv64, qnn, qrn, cn, knn, krn, wkv)


def _attn_kernel(q_ref, k_ref, v_ref, o_ref, *, tq):
    s = q_ref.shape[2]
    ri = lax.broadcasted_iota(jnp.int32, (tq, tq), 0) // CHUNK
    ci = lax.broadcasted_iota(jnp.int32, (tq, tq), 1) // CHUNK
    allowed = ci <= ri
    for qi in range(s // tq):
        lo, hi = qi * tq, (qi + 1) * tq
        q = q_ref[0, 0, lo:hi, :]
        s_d = jnp.where(allowed, _dot_nt(q, k_ref[0, 0, lo:hi, :]), -jnp.inf)
        m = jnp.max(s_d, axis=-1, keepdims=True)
        if qi > 0:
            s_o = _dot_nt(q, k_ref[0, 0, :lo, :])
            m = jnp.maximum(m, jnp.max(s_o, axis=-1, keepdims=True))
            p_o = jnp.exp(s_o - m)
            den = jnp.sum(p_o, axis=-1, keepdims=True)
            acc = _dot(p_o.astype(BF16), v_ref[0, 0, :lo, :])
        p_d = jnp.exp(s_d - m)
        if qi > 0:
            den = den + jnp.sum(p_d, axis=-1, keepdims=True)
            acc = acc + _dot(p_d.astype(BF16), v_ref[0, 0, lo:hi, :])
        else:
            den = jnp.sum(p_d, axis=-1, keepdims=True)
            acc = _dot(p_d.astype(BF16), v_ref[0, 0, lo:hi, :])
        o_ref[lo:hi, :] = (acc / den).astype(o_ref.dtype)


def _mla_attn(q, k, v):
    b, nh, s, _ = q.shape
    tq = min(256, s)
    spec = lambda w: pl.BlockSpec((1, 1, s, w), lambda bi, h: (bi, h, 0, 0))
    return pl.pallas_call(
        functools.partial(_attn_kernel, tq=tq),
        grid=(b, nh),
        in_specs=[spec(MLA_QK), spec(MLA_QK), spec(MLA_V)],
        out_specs=pl.BlockSpec((s, MLA_V), lambda bi, h: (bi, h)),
        out_shape=jax.ShapeDtypeStruct((b * s, MLA_WIDTH), BF16),
        compiler_params=_cparams("parallel", "parallel"),
        name="mla_attn",
    )(q, k, v)


def _merge_kernel(x_ref, oa_ref, ob_ref, ga_ref, gb_ref, wa_ref, wb_ref, wo_ref, o_ref):
    ya = jax.nn.sigmoid(ga_ref[...].astype(F32)) * _dot(oa_ref[...], wa_ref[...])
    yb = jax.nn.sigmoid(gb_ref[...].astype(F32)) * _dot(ob_ref[...], wb_ref[...])
    o_ref[...] = x_ref[...] + _dot((ya + yb).astype(BF16), wo_ref[...])


def _merge(x2, o_a, o_b, proj, w_a, w_b, w_o):
    t = x2.shape[0]
    tm = min(512, t)
    row = lambda w, blk: pl.BlockSpec((tm, w), lambda i: (i, blk))
    return pl.pallas_call(
        _merge_kernel,
        grid=(t // tm,),
        in_specs=[
            row(D_MODEL, 0), row(DN_WIDTH, 0), row(MLA_WIDTH, 0), row(D_MODEL, GATE_BLK0), row(D_MODEL, GATE_BLK0 + 1),
            _resident((DN_WIDTH, D_MODEL)), _resident((MLA_WIDTH, D_MODEL)), _resident((D_MODEL, D_MODEL)),
        ],
        out_specs=row(D_MODEL, 0),
        out_shape=jax.ShapeDtypeStruct((t, D_MODEL), F32),
        compiler_params=_cparams("parallel"),
        name="merge",
    )(x2, o_a, o_b, proj, proj, w_a, w_b, w_o)


def _mlp_kernel(x_ref, gm_ref, up_ref, dn_ref, gp_ref, wg_ref, p_ref, wp_ref, o_ref, hm_ref):
    f = pl.program_id(1)

    @pl.when(f == 0)
    def _():
        x = x_ref[...]
        hm_ref[...] = _rms(x, gm_ref[...]).astype(BF16)
        o_ref[...] = x

    a = _dot(hm_ref[...], up_ref[...])
    o_ref[...] += _dot(jnp.square(jnp.maximum(a, 0.0)).astype(BF16), dn_ref[...])

    @pl.when(f == pl.num_programs(1) - 1)
    def _():
        x2 = o_ref[...]
        gate = jax.nn.sigmoid(_dot(_rms(x2, gp_ref[...]).astype(BF16), wg_ref[...]))
        o_ref[...] = x2 + gate * _dot(p_ref[...].astype(BF16), wp_ref[...])


def _mlp_ple(x1, gm, w_up, w_dn, gp, w_gate, p2, w_ple):
    t = x1.shape[0]
    tm = min(512, t)
    tf = 1024
    return pl.pallas_call(
        _mlp_kernel,
        grid=(t // tm, D_FF // tf),
        in_specs=[
            pl.BlockSpec((tm, D_MODEL), lambda i, f: (i, 0)),
            _resident((1, D_MODEL)),
            pl.BlockSpec((D_MODEL, tf), lambda i, f: (0, f)),
            pl.BlockSpec((tf, D_MODEL), lambda i, f: (f, 0)),
            _resident((1, D_MODEL)),
            _resident((D_MODEL, D_MODEL)),
            pl.BlockSpec((tm, PLE_DIM), lambda i, f: (i, 0)),
            _resident((PLE_DIM, D_MODEL)),
        ],
        out_specs=pl.BlockSpec((tm, D_MODEL), lambda i, f: (i, 0)),
        out_shape=jax.ShapeDtypeStruct((t, D_MODEL), F32),
        scratch_shapes=[pltpu.VMEM((tm, D_MODEL), BF16)],
        compiler_params=_cparams("parallel", "arbitrary"),
        name="mlp_ple",
    )(x1, gm, w_up, w_dn, gp, w_gate, p2, w_ple)


def _regroup_w_in(w):
    d = w.shape[0]
    mq = w[:, _OFF_MQ:_OFF_CKV].reshape(d, MLA_HEADS, MLA_QK)
    main = jnp.concatenate([
        mq[:, :, :MLA_NOPE].reshape(d, MLA_HEADS * MLA_NOPE),
        mq[:, :, MLA_NOPE:].reshape(d, MLA_HEADS * MLA_ROPE),
        w[:, _OFF_CKV:_OFF_KR],
        w[:, _OFF_Q:_OFF_BETA],
        w[:, _OFF_GA:D_IN],
    ], axis=1).astype(BF16)
    tail = jnp.concatenate([
        w[:, _OFF_KR:_OFF_GA],
        w[:, _OFF_BETA:_OFF_MQ],
        jnp.zeros((d, LANES - MLA_ROPE - 2 * DN_HEADS), w.dtype),
    ], axis=1).astype(BF16)
    return main, tail


def kernel(x, p, positions, mix_norm, w_in, conv_w, dt_bias, a_log, dn_out_norm, ckv_norm, w_kv_up, q_nope_norm,
           q_rope_norm, k_nope_norm, k_rope_norm, w_branch_a, w_branch_b, w_out, mlp_norm, w_mlp_up, w_mlp_down,
           ple_norm, w_ple_gate, w_ple):
    b, s, d = x.shape
    t = b * s
    depth = w_in.shape[0]
    half = MLA_ROPE // 2
    inv = ROPE_BASE ** (-jnp.arange(half, dtype=F32) / half)
    inv64 = jnp.concatenate([inv, inv]).reshape(1, MLA_ROPE)
    pos = positions.reshape(t, 1)
    row = lambda a: a.reshape(1, -1).astype(F32)

    x2 = x.reshape(t, d)
    for i in range(depth):
        w_main, w_tail = _regroup_w_in(w_in[i])
        proj, tail = _in_proj(x2, row(mix_norm[i]), w_main, w_tail)
        o_a = _gdn(proj, tail, conv_w[i], a_log[i], dt_bias[i], row(dn_out_norm[i]), b, s)
        q_b, k_b, v_b = _mla_prep(proj, tail, pos, inv64, row(q_nope_norm[i]), row(q_rope_norm[i]),
                                  row(ckv_norm[i]), row(k_nope_norm[i]), row(k_rope_norm[i]),
                                  w_kv_up[i].astype(BF16), b, s)
        o_b = _mla_attn(q_b, k_b, v_b)
        x1 = _merge(x2, o_a, o_b, proj, w_branch_a[i].astype(BF16), w_branch_b[i].astype(BF16),
                    w_out[i].astype(BF16))
        x2 = _mlp_ple(x1, row(mlp_norm[i]), w_mlp_up[i].astype(BF16), w_mlp_down[i].astype(BF16),
                      row(ple_norm[i]), w_ple_gate[i].astype(BF16), p[i].reshape(t, PLE_DIM), w_ple[i].astype(BF16))
    return x2.reshape(b, s, d)
```

```python
import functools

import jax
import jax.numpy as jnp
from jax import lax
from jax.experimental import pallas as pl
from jax.experimental.pallas import tpu as pltpu

F32 = jnp.float32
BF16 = jnp.bfloat16

D_MODEL = 2048
CHUNK = 64
PLE_DIM = 256
EPS = 1e-6
DN_HEADS = 8
DN_HEAD_DIM = 128
DN_WIDTH = DN_HEADS * DN_HEAD_DIM
CONV_K = 4
MLA_HEADS = 8
MLA_NOPE = 128
MLA_ROPE = 64
MLA_QK = MLA_NOPE + MLA_ROPE
MLA_V = 128
KV_RANK = 512
MLA_WIDTH = MLA_HEADS * MLA_V
ROPE_BASE = 10000.0
D_FF = 4 * D_MODEL

_OFF_Q = 0
_OFF_K = _OFF_Q + DN_WIDTH
_OFF_V = _OFF_K + DN_WIDTH
_OFF_Z = _OFF_V + DN_WIDTH
_OFF_BETA = _OFF_Z + DN_WIDTH
_OFF_ALPHA = _OFF_BETA + DN_HEADS
_OFF_MQ = _OFF_ALPHA + DN_HEADS
_OFF_CKV = _OFF_MQ + MLA_HEADS * MLA_QK
_OFF_KR = _OFF_CKV + KV_RANK
_OFF_GA = _OFF_KR + MLA_ROPE
_OFF_GB = _OFF_GA + D_MODEL
D_IN = _OFF_GB + D_MODEL

MQ_W = MLA_HEADS * MLA_QK
CKV_BLK = MQ_W // KV_RANK
GATE_BLK0 = (MQ_W + KV_RANK + 4 * DN_WIDTH) // D_MODEL
N_MAIN = MQ_W + KV_RANK + 4 * DN_WIDTH + 2 * D_MODEL
LANES = 128
TAIL_BETA = MLA_ROPE
TAIL_ALPHA = MLA_ROPE + DN_HEADS

VMEM_LIMIT = 60 * 1024 * 1024


def _cparams(*sem):
    return pltpu.CompilerParams(dimension_semantics=sem, vmem_limit_bytes=VMEM_LIMIT)


def _rms(t, g):
    return t * lax.rsqrt(jnp.mean(t * t, axis=-1, keepdims=True) + EPS) * g


def _dot(a, b):
    return jnp.dot(a, b, preferred_element_type=F32)


def _dot_nt(a, b):
    return lax.dot_general(a, b, (((1,), (1,)), ((), ())), preferred_element_type=F32)


def _dot_tn(a, b):
    return lax.dot_general(a, b, (((0,), (0,)), ((), ())), preferred_element_type=F32)


def _resident(shape):
    nd = len(shape)
    return pl.BlockSpec(shape, lambda *_: (0,) * nd, pipeline_mode=pl.Buffered(1))


def _inproj_kernel(x_ref, g_ref, w_ref, wt_ref, o_ref, t_ref, h_ref):
    @pl.when(pl.program_id(1) == 0)
    def _():
        h = _rms(x_ref[...], g_ref[...]).astype(BF16)
        h_ref[...] = h
        t_ref[...] = _dot(h, wt_ref[...])

    o_ref[...] = _dot(h_ref[...], w_ref[...]).astype(o_ref.dtype)


def _in_proj(x2, gain, w_main, w_tail):
    t = x2.shape[0]
    tm = min(1024, t)
    tn = 1024
    return pl.pallas_call(
        _inproj_kernel,
        grid=(t // tm, N_MAIN // tn),
        in_specs=[
            pl.BlockSpec((tm, D_MODEL), lambda i, j: (i, 0)),
            _resident((1, D_MODEL)),
            pl.BlockSpec((D_MODEL, tn), lambda i, j: (0, j)),
            _resident((D_MODEL, LANES)),
        ],
        out_specs=[
            pl.BlockSpec((tm, tn), lambda i, j: (i, j)),
            pl.BlockSpec((tm, LANES), lambda i, j: (i, 0)),
        ],
        out_shape=[jax.ShapeDtypeStruct((t, N_MAIN), BF16), jax.ShapeDtypeStruct((t, LANES), F32)],
        scratch_shapes=[pltpu.VMEM((tm, D_MODEL), BF16)],
        compiler_params=_cparams("parallel", "arbitrary"),
        name="in_proj",
    )(x2, gain, w_main, w_tail)


GDN_GROUP = 4
GDN_PREP_CHUNKS = 4
CONV_PAD = 8


def _dot_r(x, m):
    return _dot(x.astype(BF16), m.astype(BF16))


def _gdn_kernel(alog_ref, dtb_ref, q_ref, k_ref, v_ref, z_ref, t_ref, cwq_ref, cwk_ref, cwv_ref, on_ref,
                o_ref,
                g_s, k_s, kq_s, kv_s, wq_s, kd_s, u_s, qk_s, st_s, pad_s):
    s = q_ref.shape[0]
    n_chunks = s // CHUNK
    hd = DN_HEAD_DIM
    group = pl.program_id(1)
    pos = lax.broadcasted_iota(jnp.int32, (s, 1), 0) % CHUNK
    lane = lax.broadcasted_iota(jnp.int32, (1, LANES), 1)

    pad_s[:CONV_PAD, :] = jnp.zeros((CONV_PAD, hd), F32)

    def conv_silu(u_ref, w_ref, c0):
        w = w_ref[:, c0:c0 + hd]
        pad_s[CONV_PAD:, :] = u_ref[:, c0:c0 + hd].astype(F32)
        acc = pad_s[CONV_PAD:, :] * w[CONV_K - 1:CONV_K, :]
        for d in range(1, CONV_K):
            acc = acc + pad_s[CONV_PAD - d:CONV_PAD - d + s, :] * w[CONV_K - 1 - d:CONV_K - d, :]
        return acc * jax.nn.sigmoid(acc)

    def l2norm(t):
        return t * lax.rsqrt(jnp.sum(t * t, axis=-1, keepdims=True) + EPS)

    def chunked(t):
        return t.reshape(n_chunks, CHUNK, t.shape[-1])

    def column(t, c):
        return jnp.sum(jnp.where(lane == c, t, 0.0), axis=1, keepdims=True)

    tail = t_ref[...]
    beta_all = jax.nn.sigmoid(tail)
    xs = tail + dtb_ref[...]
    softplus = jnp.maximum(xs, 0.0) + jnp.log1p(jnp.exp(-jnp.abs(xs)))
    gc_all = -jnp.exp(alog_ref[...]) * softplus
    step = 1
    while step < CHUNK:
        gc_all = gc_all + jnp.where(pos >= step, pltpu.roll(gc_all, step, 0), 0.0)
        step *= 2

    for hh in range(GDN_GROUP):
        head = group * GDN_GROUP + hh
        c0 = hh * hd
        qn = l2norm(conv_silu(q_ref, cwq_ref, c0))
        kn = l2norm(conv_silu(k_ref, cwk_ref, c0))
        vv = conv_silu(v_ref, cwv_ref, c0)

        beta = column(beta_all, TAIL_BETA + head)
        gc = column(gc_all, TAIL_ALPHA + head)
        gc3 = chunked(gc)
        g_last = jnp.broadcast_to(gc3[:, CHUNK - 1:CHUNK, :], gc3.shape).reshape(s, 1)
        eg = jnp.exp(gc)

        qs = qn * (hd ** -0.5)
        kb = kn * beta
        g_s[hh] = jnp.broadcast_to(gc, (s, LANES))
        k_s[hh] = kn.astype(BF16)
        kq_s[hh, :, :CHUNK, :] = chunked(kb.astype(BF16))
        kq_s[hh, :, CHUNK:, :] = chunked(qs.astype(BF16))
        kv_s[hh, :, :hd] = (kb * eg).astype(BF16)
        kv_s[hh, :, hd:] = (vv * beta).astype(BF16)
        wq_s[hh, :, CHUNK:, :] = chunked((qs * eg).astype(BF16))
        kd_s[hh] = (kn * jnp.exp(g_last - gc)).astype(BF16)
        st_s[hh] = jnp.zeros((hd, hd), F32)

    ri = lax.broadcasted_iota(jnp.int32, (CHUNK, CHUNK), 0)
    ci = lax.broadcasted_iota(jnp.int32, (CHUNK, CHUNK), 1)
    eye = ri == ci
    eye_f = jnp.where(eye, 1.0, 0.0).astype(F32)
    n_factors = CHUNK.bit_length() - 2

    heads = range(GDN_GROUP)

    def prep(it):
        items = [(hh, it * GDN_PREP_CHUNKS + c) for c in range(GDN_PREP_CHUNKS) for hh in heads]
        sls = [pl.ds(pl.multiple_of(n * CHUNK, CHUNK), CHUNK) for _, n in items]
        rs = [_dot_nt(kq_s[hh, n], k_s[hh, sl, :]) for (hh, n), sl in zip(items, sls)]
        yield
        ms, ps = [], []
        for (hh, n), sl, r in zip(items, sls, rs):
            g_col = g_s[hh, sl, :][:, :CHUNK]
            g_row = jnp.sum(jnp.where(eye, g_col, 0.0), axis=0, keepdims=True)
            decay = jnp.exp(jnp.where(ri >= ci, g_col - g_row, -jnp.inf))
            a = jnp.where(ri > ci, r[:CHUNK] * decay, 0.0)
            qk_s[hh, sl, :] = (r[CHUNK:] * decay).astype(BF16)
            ms.append(a)
            ps.append(eye_f - a)
        ms = [_dot_r(a, a) for a in ms]
        yield
        for _ in range(n_factors - 1):
            rs = [_dot_r(jnp.concatenate([m, p], axis=0), m) for m, p in zip(ms, ps)]
            ms = [r[:CHUNK] for r in rs]
            ps = [p + r[CHUNK:] for p, r in zip(ps, rs)]
            yield
        rs = [_dot_r(p, m) for m, p in zip(ms, ps)]
        yield
        wus = [_dot((p + r).astype(BF16), kv_s[hh, sl, :]) for (hh, _), sl, p, r in zip(items, sls, ps, rs)]
        for (hh, n), sl, wu in zip(items, sls, wus):
            wq_s[hh, n, :CHUNK, :] = wu[:, :hd].astype(BF16)
            u_s[hh, sl, :] = wu[:, hd:]
        yield

    def scan(n):
        r0 = pl.multiple_of(n * CHUNK, CHUNK)
        sl = pl.ds(r0, CHUNK)
        states = [st_s[hh] for hh in heads]
        rs = [_dot(wq_s[hh, n], states[hh].astype(BF16)) for hh in heads]
        yield
        v_nbs = [(u_s[hh, sl, :] - rs[hh][:CHUNK]).astype(BF16) for hh in heads]
        os = [rs[hh][CHUNK:] + _dot(qk_s[hh, sl, :], v_nbs[hh]) for hh in heads]
        kvs = [_dot_tn(kd_s[hh, sl, :], v_nbs[hh]) for hh in heads]
        yield
        for hh in heads:
            decay_last = jnp.exp(g_s[hh, pl.ds(r0 + CHUNK - 1, 1), :])
            st_s[hh] = states[hh] * decay_last + kvs[hh]
            z = z_ref[sl, hh * hd:(hh + 1) * hd].astype(F32)
            o_ref[sl, hh * hd:(hh + 1) * hd] = (_rms(os[hh], on_ref[...]) * (z * jax.nn.sigmoid(z))).astype(o_ref.dtype)

    n_groups = n_chunks // GDN_PREP_CHUNKS
    for _ in prep(0):
        pass

    def body(it, carry):
        prep_stages = prep((it + 1) % n_groups)
        for c in range(GDN_PREP_CHUNKS):
            for _ in scan(it * GDN_PREP_CHUNKS + c):
                next(prep_stages, None)
        for _ in prep_stages:
            pass
        return carry

    lax.fori_loop(0, n_groups, body, 0)


def _gdn(proj, tail, conv_w, a_log, dt_bias, out_norm, b, s):
    hd = DN_HEAD_DIM
    gw = hd * GDN_GROUP
    n_groups = DN_HEADS // GDN_GROUP
    n_chunks = s // CHUNK
    blk0 = (MQ_W + KV_RANK) // gw
    col = lambda k: pl.BlockSpec((s, gw), lambda bi, g: (bi, blk0 + k * n_groups + g))
    cw = lambda k: pl.BlockSpec((CONV_K, gw), lambda bi, g: (0, k * n_groups + g))
    per_head = lambda shape, dt: pltpu.VMEM((GDN_GROUP,) + shape, dt)
    lane_row = lambda a: jnp.zeros((1, LANES), F32).at[0, TAIL_ALPHA:TAIL_ALPHA + DN_HEADS].set(a.astype(F32))
    return pl.pallas_call(
        _gdn_kernel,
        grid=(b, n_groups),
        in_specs=[
            _resident((1, LANES)), _resident((1, LANES)),
            col(0), col(1), col(2), col(3),
            pl.BlockSpec((s, LANES), lambda bi, g: (bi, 0)),
            cw(0), cw(1), cw(2),
            _resident((1, hd)),
        ],
        out_specs=pl.BlockSpec((s, gw), lambda bi, g: (bi, g)),
        out_shape=jax.ShapeDtypeStruct((b * s, DN_WIDTH), BF16),
        scratch_shapes=[
            per_head((s, LANES), F32),
            per_head((s, hd), BF16),
            per_head((n_chunks, 2 * CHUNK, hd), BF16),
            per_head((s, 2 * hd), BF16),
            per_head((n_chunks, 2 * CHUNK, hd), BF16),
            per_head((s, hd), BF16),
            per_head((s, hd), F32),
            per_head((s, CHUNK), BF16),
            per_head((hd, hd), F32),
            pltpu.VMEM((CONV_PAD + s, hd), F32),
        ],
        compiler_params=_cparams("parallel", "arbitrary"),
        name="gdn",
    )(lane_row(a_log), lane_row(dt_bias), proj, proj, proj, proj, tail, conv_w, conv_w, conv_w, out_norm)


def _mla_prep_kernel(mq_ref, ckv_ref, t_ref, pos_ref, inv_ref, qnn_ref, qrn_ref, cn_ref, knn_ref, krn_ref, wkv_ref,
                     q_out, k_out, v_out):
    half = MLA_ROPE // 2
    ang = pos_ref[...].astype(F32) * inv_ref[...]
    cos = jnp.cos(ang)
    sin = jnp.sin(ang)

    def rope(t):
        rot = jnp.concatenate([-t[:, half:], t[:, :half]], axis=1)
        return t * cos + rot * sin

    scale = MLA_QK ** -0.5
    kr = rope(_rms(t_ref[...][:, :MLA_ROPE], krn_ref[...])).astype(BF16)
    c = _rms(ckv_ref[...].astype(F32), cn_ref[...]).astype(BF16)
    kv = _dot(c, wkv_ref[...])
    mq = mq_ref[...].astype(F32)
    rope0 = MLA_HEADS * MLA_NOPE
    for h in range(MLA_HEADS):
        qn = _rms(mq[:, h * MLA_NOPE:(h + 1) * MLA_NOPE], qnn_ref[...]) * scale
        qr = rope(_rms(mq[:, rope0 + h * MLA_ROPE:rope0 + (h + 1) * MLA_ROPE], qrn_ref[...])) * scale
        q_out[0, h, :, :MLA_NOPE] = qn.astype(BF16)
        q_out[0, h, :, MLA_NOPE:] = qr.astype(BF16)
        kv0 = h * (MLA_NOPE + MLA_V)
        k_out[0, h, :, :MLA_NOPE] = _rms(kv[:, kv0:kv0 + MLA_NOPE], knn_ref[...]).astype(BF16)
        k_out[0, h, :, MLA_NOPE:] = kr
        v_out[0, h, :, :] = kv[:, kv0 + MLA_NOPE:kv0 + MLA_NOPE + MLA_V].astype(BF16)


def _mla_prep(proj, tail, pos, inv64, qnn, qrn, cn, knn, krn, wkv, b, s):
    tm = min(512, s)
    nsb = s // tm
    nh = MLA_HEADS
    row = lambda w, blk: pl.BlockSpec((tm, w), lambda i: (i, blk))
    outspec = lambda w: pl.BlockSpec((1, nh, tm, w), lambda i: (i // nsb, 0, i % nsb, 0))
    return pl.pallas_call(
        _mla_prep_kernel,
        grid=(b * nsb,),
        in_specs=[
            row(MQ_W, 0), row(KV_RANK, CKV_BLK), row(LANES, 0), row(1, 0),
            _resident((1, MLA_ROPE)), _resident((1, MLA_NOPE)), _resident((1, MLA_ROPE)),
            _resident((1, KV_RANK)), _resident((1, MLA_NOPE)), _resident((1, MLA_ROPE)),
            _resident((KV_RANK, nh * (MLA_NOPE + MLA_V))),
        ],
        out_specs=[outspec(MLA_QK), outspec(MLA_QK), outspec(MLA_V)],
        out_shape=[
            jax.ShapeDtypeStruct((b, nh, s, MLA_QK), BF16),
            jax.ShapeDtypeStruct((b, nh, s, MLA_QK), BF16),
            jax.ShapeDtypeStruct((b, nh, s, MLA_V), BF16),
        ],
        compiler_params=_cparams("parallel"),
        name="mla_prep",
    )(proj, proj, tail, pos, inv64, qnn, qrn, cn, knn, krn, wkv)


def _attn_kernel(q_ref, k_ref, v_ref, o_ref, *, tq):
    s = q_ref.shape[2]
    ri = lax.broadcasted_iota(jnp.int32, (tq, tq), 0) // CHUNK
    ci = lax.broadcasted_iota(jnp.int32, (tq, tq), 1) // CHUNK
    allowed = ci <= ri
    for qi in range(s // tq):
        lo, hi = qi * tq, (qi + 1) * tq
        q = q_ref[0, 0, lo:hi, :]
        s_d = jnp.where(allowed, _dot_nt(q, k_ref[0, 0, lo:hi, :]), -jnp.inf)
        m = jnp.max(s_d, axis=-1, keepdims=True)
        if qi > 0:
            s_o = _dot_nt(q, k_ref[0, 0, :lo, :])
            m = jnp.maximum(m, jnp.max(s_o, axis=-1, keepdims=True))
            p_o = jnp.exp(s_o - m)
            den = jnp.sum(p_o, axis=-1, keepdims=True)
            acc = _dot(p_o.astype(BF16), v_ref[0, 0, :lo, :])
        p_d = jnp.exp(s_d - m)
        if qi > 0:
            den = den + jnp.sum(p_d, axis=-1, keepdims=True)
            acc = acc + _dot(p_d.astype(BF16), v_ref[0, 0, lo:hi, :])
        else:
            den = jnp.sum(p_d, axis=-1, keepdims=True)
            acc = _dot(p_d.astype(BF16), v_ref[0, 0, lo:hi, :])
        o_ref[lo:hi, :] = (acc / den).astype(o_ref.dtype)


def _mla_attn(q, k, v):
    b, nh, s, _ = q.shape
    tq = min(256, s)
    spec = lambda w: pl.BlockSpec((1, 1, s, w), lambda bi, h: (bi, h, 0, 0))
    return pl.pallas_call(
        functools.partial(_attn_kernel, tq=tq),
        grid=(b, nh),
        in_specs=[spec(MLA_QK), spec(MLA_QK), spec(MLA_V)],
        out_specs=pl.BlockSpec((s, MLA_V), lambda bi, h: (bi, h)),
        out_shape=jax.ShapeDtypeStruct((b * s, MLA_WIDTH), BF16),
        compiler_params=_cparams("parallel", "parallel"),
        name="mla_attn",
    )(q, k, v)


def _merge_kernel(x_ref, oa_ref, ob_ref, ga_ref, gb_ref, wa_ref, wb_ref, wo_ref, o_ref):
    ya = jax.nn.sigmoid(ga_ref[...].astype(F32)) * _dot(oa_ref[...], wa_ref[...])
    yb = jax.nn.sigmoid(gb_ref[...].astype(F32)) * _dot(ob_ref[...], wb_ref[...])
    o_ref[...] = x_ref[...] + _dot((ya + yb).astype(BF16), wo_ref[...])


def _merge(x2, o_a, o_b, proj, w_a, w_b, w_o):
    t = x2.shape[0]
    tm = min(512, t)
    row = lambda w, blk: pl.BlockSpec((tm, w), lambda i: (i, blk))
    return pl.pallas_call(
        _merge_kernel,
        grid=(t // tm,),
        in_specs=[
            row(D_MODEL, 0), row(DN_WIDTH, 0), row(MLA_WIDTH, 0), row(D_MODEL, GATE_BLK0), row(D_MODEL, GATE_BLK0 + 1),
            _resident((DN_WIDTH, D_MODEL)), _resident((MLA_WIDTH, D_MODEL)), _resident((D_MODEL, D_MODEL)),
        ],
        out_specs=row(D_MODEL, 0),
        out_shape=jax.ShapeDtypeStruct((t, D_MODEL), F32),
        compiler_params=_cparams("parallel"),
        name="merge",
    )(x2, o_a, o_b, proj, proj, w_a, w_b, w_o)


def _mlp_kernel(x_ref, gm_ref, up_ref, dn_ref, gp_ref, wg_ref, p_ref, wp_ref, o_ref, hm_ref):
    f = pl.program_id(1)

    @pl.when(f == 0)
    def _():
        x = x_ref[...]
        hm_ref[...] = _rms(x, gm_ref[...]).astype(BF16)
        o_ref[...] = x

    a = _dot(hm_ref[...], up_ref[...])
    o_ref[...] += _dot(jnp.square(jnp.maximum(a, 0.0)).astype(BF16), dn_ref[...])

    @pl.when(f == pl.num_programs(1) - 1)
    def _():
        x2 = o_ref[...]
        gate = jax.nn.sigmoid(_dot(_rms(x2, gp_ref[...]).astype(BF16), wg_ref[...]))
        o_ref[...] = x2 + gate * _dot(p_ref[...].astype(BF16), wp_ref[...])


def _mlp_ple(x1, gm, w_up, w_dn, gp, w_gate, p2, w_ple):
    t = x1.shape[0]
    tm = min(512, t)
    tf = 1024
    return pl.pallas_call(
        _mlp_kernel,
        grid=(t // tm, D_FF // tf),
        in_specs=[
            pl.BlockSpec((tm, D_MODEL), lambda i, f: (i, 0)),
            _resident((1, D_MODEL)),
            pl.BlockSpec((D_MODEL, tf), lambda i, f: (0, f)),
            pl.BlockSpec((tf, D_MODEL), lambda i, f: (f, 0)),
            _resident((1, D_MODEL)),
            _resident((D_MODEL, D_MODEL)),
            pl.BlockSpec((tm, PLE_DIM), lambda i, f: (i, 0)),
            _resident((PLE_DIM, D_MODEL)),
        ],
        out_specs=pl.BlockSpec((tm, D_MODEL), lambda i, f: (i, 0)),
        out_shape=jax.ShapeDtypeStruct((t, D_MODEL), F32),
        scratch_shapes=[pltpu.VMEM((tm, D_MODEL), BF16)],
        compiler_params=_cparams("parallel", "arbitrary"),
        name="mlp_ple",
    )(x1, gm, w_up, w_dn, gp, w_gate, p2, w_ple)


def _regroup_w_in(w):
    d = w.shape[0]
    mq = w[:, _OFF_MQ:_OFF_CKV].reshape(d, MLA_HEADS, MLA_QK)
    main = jnp.concatenate([
        mq[:, :, :MLA_NOPE].reshape(d, MLA_HEADS * MLA_NOPE),
        mq[:, :, MLA_NOPE:].reshape(d, MLA_HEADS * MLA_ROPE),
        w[:, _OFF_CKV:_OFF_KR],
        w[:, _OFF_Q:_OFF_BETA],
        w[:, _OFF_GA:D_IN],
    ], axis=1).astype(BF16)
    tail = jnp.concatenate([
        w[:, _OFF_KR:_OFF_GA],
        w[:, _OFF_BETA:_OFF_MQ],
        jnp.zeros((d, LANES - MLA_ROPE - 2 * DN_HEADS), w.dtype),
    ], axis=1).astype(BF16)
    return main, tail


def kernel(x, p, positions, mix_norm, w_in, conv_w, dt_bias, a_log, dn_out_norm, ckv_norm, w_kv_up, q_nope_norm,
           q_rope_norm, k_nope_norm, k_rope_norm, w_branch_a, w_branch_b, w_out, mlp_norm, w_mlp_up, w_mlp_down,
           ple_norm, w_ple_gate, w_ple):
    b, s, d = x.shape
    t = b * s
    depth = w_in.shape[0]
    half = MLA_ROPE // 2
    inv = ROPE_BASE ** (-jnp.arange(half, dtype=F32) / half)
    inv64 = jnp.concatenate([inv, inv]).reshape(1, MLA_ROPE)
    pos = positions.reshape(t, 1)
    row = lambda a: a.reshape(1, -1).astype(F32)

    x2 = x.reshape(t, d)
    for i in range(depth):
        w_main, w_tail = _regroup_w_in(w_in[i])
        proj, tail = _in_proj(x2, row(mix_norm[i]), w_main, w_tail)
        o_a = _gdn(proj, tail, conv_w[i], a_log[i], dt_bias[i], row(dn_out_norm[i]), b, s)
        q_b, k_b, v_b = _mla_prep(proj, tail, pos, inv64, row(q_nope_norm[i]), row(q_rope_norm[i]),
                                  row(ckv_norm[i]), row(k_nope_norm[i]), row(k_rope_norm[i]),
                                  w_kv_up[i].astype(BF16), b, s)
        o_b = _mla_attn(q_b, k_b, v_b)
        x1 = _merge(x2, o_a, o_b, proj, w_branch_a[i].astype(BF16), w_branch_b[i].astype(BF16),
                    w_out[i].astype(BF16))
        x2 = _mlp_ple(x1, row(mlp_norm[i]), w_mlp_up[i].astype(BF16), w_mlp_down[i].astype(BF16),
                      row(ple_norm[i]), w_ple_gate[i].astype(BF16), p[i].reshape(t, PLE_DIM), w_ple[i].astype(BF16))
    return x2.reshape(b, s, d)
```

```python
import functools

import jax
import jax.numpy as jnp
from jax import lax
from jax.experimental import pallas as pl
from jax.experimental.pallas import tpu as pltpu

F32 = jnp.float32
BF16 = jnp.bfloat16

D_MODEL = 2048
CHUNK = 64
PLE_DIM = 256
EPS = 1e-6
DN_HEADS = 8
DN_HEAD_DIM = 128
DN_WIDTH = DN_HEADS * DN_HEAD_DIM
CONV_K = 4
MLA_HEADS = 8
MLA_NOPE = 128
MLA_ROPE = 64
MLA_QK = MLA_NOPE + MLA_ROPE
MLA_V = 128
KV_RANK = 512
MLA_WIDTH = MLA_HEADS * MLA_V
ROPE_BASE = 10000.0
D_FF = 4 * D_MODEL
LOG2_E = 1.4426950408889634

_OFF_Q = 0
_OFF_K = _OFF_Q + DN_WIDTH
_OFF_V = _OFF_K + DN_WIDTH
_OFF_Z = _OFF_V + DN_WIDTH
_OFF_BETA = _OFF_Z + DN_WIDTH
_OFF_ALPHA = _OFF_BETA + DN_HEADS
_OFF_MQ = _OFF_ALPHA + DN_HEADS
_OFF_CKV = _OFF_MQ + MLA_HEADS * MLA_QK
_OFF_KR = _OFF_CKV + KV_RANK
_OFF_GA = _OFF_KR + MLA_ROPE
_OFF_GB = _OFF_GA + D_MODEL
D_IN = _OFF_GB + D_MODEL

MQ_W = MLA_HEADS * MLA_QK
CKV_BLK = MQ_W // KV_RANK
GATE_BLK0 = (MQ_W + KV_RANK + 4 * DN_WIDTH) // D_MODEL
N_MAIN = MQ_W + KV_RANK + 4 * DN_WIDTH + 2 * D_MODEL
LANES = 128
TAIL_BETA = MLA_ROPE
TAIL_ALPHA = MLA_ROPE + DN_HEADS

VMEM_LIMIT = 60 * 1024 * 1024


def _cparams(*sem):
    return pltpu.CompilerParams(dimension_semantics=sem, vmem_limit_bytes=VMEM_LIMIT)


def _rms(t, g):
    return t * lax.rsqrt(jnp.mean(t * t, axis=-1, keepdims=True) + EPS) * g


def _dot(a, b):
    return jnp.dot(a, b, preferred_element_type=F32)


def _dot_nt(a, b):
    return lax.dot_general(a, b, (((1,), (1,)), ((), ())), preferred_element_type=F32)


def _dot_tn(a, b):
    return lax.dot_general(a, b, (((0,), (0,)), ((), ())), preferred_element_type=F32)


def _resident(shape):
    nd = len(shape)
    return pl.BlockSpec(shape, lambda *_: (0,) * nd, pipeline_mode=pl.Buffered(1))


def _inproj_kernel(x_ref, g_ref, w_ref, wt_ref, o_ref, t_ref, h_ref):
    @pl.when(pl.program_id(1) == 0)
    def _():
        h = _rms(x_ref[...], g_ref[...]).astype(BF16)
        h_ref[...] = h
        t_ref[...] = _dot(h, wt_ref[...])

    o_ref[...] = _dot(h_ref[...], w_ref[...]).astype(o_ref.dtype)


def _in_proj(x2, gain, w_main, w_tail):
    t = x2.shape[0]
    tm = min(1024, t)
    tn = 2048
    return pl.pallas_call(
        _inproj_kernel,
        grid=(t // tm, N_MAIN // tn),
        in_specs=[
            pl.BlockSpec((tm, D_MODEL), lambda i, j: (i, 0)),
            _resident((1, D_MODEL)),
            pl.BlockSpec((D_MODEL, tn), lambda i, j: (0, j)),
            _resident((D_MODEL, LANES)),
        ],
        out_specs=[
            pl.BlockSpec((tm, tn), lambda i, j: (i, j)),
            pl.BlockSpec((tm, LANES), lambda i, j: (i, 0)),
        ],
        out_shape=[jax.ShapeDtypeStruct((t, N_MAIN), BF16), jax.ShapeDtypeStruct((t, LANES), F32)],
        scratch_shapes=[pltpu.VMEM((tm, D_MODEL), BF16)],
        compiler_params=_cparams("parallel", "arbitrary"),
        name="in_proj",
    )(x2, gain, w_main, w_tail)


GDN_GROUP = 4
GDN_PREP_CHUNKS = 4
CONV_PAD = 8


def _dot_r(x, m):
    return _dot(x.astype(BF16), m.astype(BF16))


def _gdn_kernel(alog_ref, dtb_ref, q_ref, k_ref, v_ref, z_ref, t_ref, cwq_ref, cwk_ref, cwv_ref, on_ref,
                o_ref,
                g_s, k_s, kq_s, kv_s, wq_s, kd_s, u_s, qk_s, st_s, pad_s):
    s = q_ref.shape[0]
    n_chunks = s // CHUNK
    hd = DN_HEAD_DIM
    group = pl.program_id(1)
    pos = lax.broadcasted_iota(jnp.int32, (s, 1), 0) % CHUNK
    lane = lax.broadcasted_iota(jnp.int32, (1, LANES), 1)

    pad_s[:CONV_PAD, :] = jnp.zeros((CONV_PAD, hd), F32)

    def conv_silu(u_ref, w_ref, c0):
        w = w_ref[:, c0:c0 + hd]
        pad_s[CONV_PAD:, :] = u_ref[:, c0:c0 + hd].astype(F32)
        acc = pad_s[CONV_PAD:, :] * w[CONV_K - 1:CONV_K, :]
        for d in range(1, CONV_K):
            acc = acc + pad_s[CONV_PAD - d:CONV_PAD - d + s, :] * w[CONV_K - 1 - d:CONV_K - d, :]
        return acc * jax.nn.sigmoid(acc)

    def l2norm(t):
        return t * lax.rsqrt(jnp.sum(t * t, axis=-1, keepdims=True) + EPS)

    def chunked(t):
        return t.reshape(n_chunks, CHUNK, t.shape[-1])

    def column(t, c):
        return jnp.sum(jnp.where(lane == c, t, 0.0), axis=1, keepdims=True)

    tail = t_ref[...]
    beta_all = jax.nn.sigmoid(tail)
    xs = tail + dtb_ref[...]
    softplus = jnp.maximum(xs, 0.0) + jnp.log1p(jnp.exp(-jnp.abs(xs)))
    gc_all = -jnp.exp(alog_ref[...]) * softplus
    step = 1
    while step < CHUNK:
        gc_all = gc_all + jnp.where(pos >= step, pltpu.roll(gc_all, step, 0), 0.0)
        step *= 2

    for hh in range(GDN_GROUP):
        head = group * GDN_GROUP + hh
        c0 = hh * hd
        qn = l2norm(conv_silu(q_ref, cwq_ref, c0))
        kn = l2norm(conv_silu(k_ref, cwk_ref, c0))
        vv = conv_silu(v_ref, cwv_ref, c0)

        beta = column(beta_all, TAIL_BETA + head)
        gc = column(gc_all, TAIL_ALPHA + head)
        gc3 = chunked(gc)
        g_last = jnp.broadcast_to(gc3[:, CHUNK - 1:CHUNK, :], gc3.shape).reshape(s, 1)
        eg = jnp.exp(gc)

        qs = qn * (hd ** -0.5)
        kb = kn * beta
        g_s[hh] = jnp.broadcast_to(gc, (s, LANES))
        k_s[hh] = kn.astype(BF16)
        kq_s[hh, :, :CHUNK, :] = chunked(kb.astype(BF16))
        kq_s[hh, :, CHUNK:, :] = chunked(qs.astype(BF16))
        kv_s[hh, :, :hd] = (kb * eg).astype(BF16)
        kv_s[hh, :, hd:] = (vv * beta).astype(BF16)
        wq_s[hh, :, CHUNK:, :] = chunked((qs * eg).astype(BF16))
        kd_s[hh] = (kn * jnp.exp(g_last - gc)).astype(BF16)
        st_s[hh] = jnp.zeros((hd, hd), F32)

    ri = lax.broadcasted_iota(jnp.int32, (CHUNK, CHUNK), 0)
    ci = lax.broadcasted_iota(jnp.int32, (CHUNK, CHUNK), 1)
    eye = ri == ci
    eye_f = jnp.where(eye, 1.0, 0.0).astype(F32)
    n_factors = CHUNK.bit_length() - 2

    heads = range(GDN_GROUP)

    def prep(it):
        items = [(hh, it * GDN_PREP_CHUNKS + c) for c in range(GDN_PREP_CHUNKS) for hh in heads]
        sls = [pl.ds(pl.multiple_of(n * CHUNK, CHUNK), CHUNK) for _, n in items]
        rs = [_dot_nt(kq_s[hh, n], k_s[hh, sl, :]) for (hh, n), sl in zip(items, sls)]
        yield
        ms, ps = [], []
        for (hh, n), sl, r in zip(items, sls, rs):
            g_col = g_s[hh, sl, :][:, :CHUNK]
            g_row = jnp.sum(jnp.where(eye, g_col, 0.0), axis=0, keepdims=True)
            decay = jnp.exp(jnp.where(ri >= ci, g_col - g_row, -jnp.inf))
            a = jnp.where(ri > ci, r[:CHUNK] * decay, 0.0)
            qk_s[hh, sl, :] = (r[CHUNK:] * decay).astype(BF16)
            ms.append(a)
            ps.append(eye_f - a)
        ms = [_dot_r(a, a) for a in ms]
        yield
        for _ in range(n_factors - 1):
            rs = [_dot_r(jnp.concatenate([m, p], axis=0), m) for m, p in zip(ms, ps)]
            ms = [r[:CHUNK] for r in rs]
            ps = [p + r[CHUNK:] for p, r in zip(ps, rs)]
            yield
        rs = [_dot_r(p, m) for m, p in zip(ms, ps)]
        yield
        wus = [_dot((p + r).astype(BF16), kv_s[hh, sl, :]) for (hh, _), sl, p, r in zip(items, sls, ps, rs)]
        for (hh, n), sl, wu in zip(items, sls, wus):
            wq_s[hh, n, :CHUNK, :] = wu[:, :hd].astype(BF16)
            u_s[hh, sl, :] = wu[:, hd:]
        yield

    def scan(n):
        r0 = pl.multiple_of(n * CHUNK, CHUNK)
        sl = pl.ds(r0, CHUNK)
        states = [st_s[hh] for hh in heads]
        rs = [_dot(wq_s[hh, n], states[hh].astype(BF16)) for hh in heads]
        yield
        v_nbs = [(u_s[hh, sl, :] - rs[hh][:CHUNK]).astype(BF16) for hh in heads]
        os = [rs[hh][CHUNK:] + _dot(qk_s[hh, sl, :], v_nbs[hh]) for hh in heads]
        kvs = [_dot_tn(kd_s[hh, sl, :], v_nbs[hh]) for hh in heads]
        yield
        for hh in heads:
            decay_last = jnp.exp(g_s[hh, pl.ds(r0 + CHUNK - 1, 1), :])
            st_s[hh] = states[hh] * decay_last + kvs[hh]
            z = z_ref[sl, hh * hd:(hh + 1) * hd].astype(F32)
            o_ref[sl, hh * hd:(hh + 1) * hd] = (_rms(os[hh], on_ref[...]) * (z * jax.nn.sigmoid(z))).astype(o_ref.dtype)

    n_groups = n_chunks // GDN_PREP_CHUNKS
    for _ in prep(0):
        pass

    def body(it, carry):
        prep_stages = prep((it + 1) % n_groups)
        for c in range(GDN_PREP_CHUNKS):
            for _ in scan(it * GDN_PREP_CHUNKS + c):
                next(prep_stages, None)
        for _ in prep_stages:
            pass
        return carry

    lax.fori_loop(0, n_groups, body, 0)


def _gdn(proj, tail, conv_w, a_log, dt_bias, out_norm, b, s):
    hd = DN_HEAD_DIM
    gw = hd * GDN_GROUP
    n_groups = DN_HEADS // GDN_GROUP
    n_chunks = s // CHUNK
    blk0 = (MQ_W + KV_RANK) // gw
    col = lambda k: pl.BlockSpec((s, gw), lambda bi, g: (bi, blk0 + k * n_groups + g))
    cw = lambda k: pl.BlockSpec((CONV_K, gw), lambda bi, g: (0, k * n_groups + g))
    per_head = lambda shape, dt: pltpu.VMEM((GDN_GROUP,) + shape, dt)
    lane_row = lambda a: jnp.zeros((1, LANES), F32).at[0, TAIL_ALPHA:TAIL_ALPHA + DN_HEADS].set(a.astype(F32))
    return pl.pallas_call(
        _gdn_kernel,
        grid=(b, n_groups),
        in_specs=[
            _resident((1, LANES)), _resident((1, LANES)),
            col(0), col(1), col(2), col(3),
            pl.BlockSpec((s, LANES), lambda bi, g: (bi, 0)),
            cw(0), cw(1), cw(2),
            _resident((1, hd)),
        ],
        out_specs=pl.BlockSpec((s, gw), lambda bi, g: (bi, g)),
        out_shape=jax.ShapeDtypeStruct((b * s, DN_WIDTH), BF16),
        scratch_shapes=[
            per_head((s, LANES), F32),
            per_head((s, hd), BF16),
            per_head((n_chunks, 2 * CHUNK, hd), BF16),
            per_head((s, 2 * hd), BF16),
            per_head((n_chunks, 2 * CHUNK, hd), BF16),
            per_head((s, hd), BF16),
            per_head((s, hd), F32),
            per_head((s, CHUNK), BF16),
            per_head((hd, hd), F32),
            pltpu.VMEM((CONV_PAD + s, hd), F32),
        ],
        compiler_params=_cparams("parallel", "arbitrary"),
        name="gdn",
    )(lane_row(a_log), lane_row(dt_bias), proj, proj, proj, proj, tail, conv_w, conv_w, conv_w, out_norm)


def _mla_prep_kernel(mq_ref, ckv_ref, t_ref, pos_ref, inv_ref, qnn_ref, qrn_ref, cn_ref, knn_ref, krn_ref, wkv_ref,
                     q_out, k_out, v_out):
    half = MLA_ROPE // 2
    ang = pos_ref[...].astype(F32) * inv_ref[...]
    cos = jnp.cos(ang)
    sin = jnp.sin(ang)

    def rope(t):
        rot = jnp.concatenate([-t[:, half:], t[:, :half]], axis=1)
        return t * cos + rot * sin

    scale = MLA_QK ** -0.5 * LOG2_E
    kr = rope(_rms(t_ref[...][:, :MLA_ROPE], krn_ref[...])).astype(BF16)
    c = _rms(ckv_ref[...].astype(F32), cn_ref[...]).astype(BF16)
    kv = _dot(c, wkv_ref[...])
    mq = mq_ref[...].astype(F32)
    rope0 = MLA_HEADS * MLA_NOPE
    for h in range(MLA_HEADS):
        qn = _rms(mq[:, h * MLA_NOPE:(h + 1) * MLA_NOPE], qnn_ref[...]) * scale
        qr = rope(_rms(mq[:, rope0 + h * MLA_ROPE:rope0 + (h + 1) * MLA_ROPE], qrn_ref[...])) * scale
        q_out[0, h, :, :MLA_NOPE] = qn.astype(BF16)
        q_out[0, h, :, MLA_NOPE:] = qr.astype(BF16)
        kv0 = h * (MLA_NOPE + MLA_V)
        k_out[0, h, :, :MLA_NOPE] = _rms(kv[:, kv0:kv0 + MLA_NOPE], knn_ref[...]).astype(BF16)
        k_out[0, h, :, MLA_NOPE:] = kr
        v_out[0, h, :, :] = kv[:, kv0 + MLA_NOPE:kv0 + MLA_NOPE + MLA_V].astype(BF16)


def _mla_prep(proj, tail, pos, inv64, qnn, qrn, cn, knn, krn, wkv, b, s):
    tm = min(512, s)
    nsb = s // tm
    nh = MLA_HEADS
    row = lambda w, blk: pl.BlockSpec((tm, w), lambda i: (i, blk))
    outspec = lambda w: pl.BlockSpec((1, nh, tm, w), lambda i: (i // nsb, 0, i % nsb, 0))
    return pl.pallas_call(
        _mla_prep_kernel,
        grid=(b * nsb,),
        in_specs=[
            row(MQ_W, 0), row(KV_RANK, CKV_BLK), row(LANES, 0), row(1, 0),
            _resident((1, MLA_ROPE)), _resident((1, MLA_NOPE)), _resident((1, MLA_ROPE)),
            _resident((1, KV_RANK)), _resident((1, MLA_NOPE)), _resident((1, MLA_ROPE)),
            _resident((KV_RANK, nh * (MLA_NOPE + MLA_V))),
        ],
        out_specs=[outspec(MLA_QK), outspec(MLA_QK), outspec(MLA_V)],
        out_shape=[
            jax.ShapeDtypeStruct((b, nh, s, MLA_QK), BF16),
            jax.ShapeDtypeStruct((b, nh, s, MLA_QK), BF16),
            jax.ShapeDtypeStruct((b, nh, s, MLA_V), BF16),
        ],
        compiler_params=_cparams("parallel"),
        name="mla_prep",
    )(proj, proj, tail, pos, inv64, qnn, qrn, cn, knn, krn, wkv)


ATTN_LOOKAHEAD = 1


def _attn_kernel(q_ref, k_ref, v_ref, o_ref, v1_s, *, tq):
    s = q_ref.shape[2]
    v1_s[:, :MLA_V] = v_ref[0, 0]
    v1_s[:, MLA_V:] = jnp.ones((s, MLA_V), BF16)
    ri = lax.broadcasted_iota(jnp.int32, (tq, tq), 0) // CHUNK
    ci = lax.broadcasted_iota(jnp.int32, (tq, tq), 1) // CHUNK
    allowed = ci <= ri
    n_q = s // tq

    def scores(qi):
        lo, hi = qi * tq, (qi + 1) * tq
        q = q_ref[0, 0, lo:hi, :]
        s_d = jnp.where(allowed, _dot_nt(q, k_ref[0, 0, lo:hi, :]), -jnp.inf)
        s_o = _dot_nt(q, k_ref[0, 0, :lo, :]) if qi > 0 else None
        return s_d, s_o

    pending = [scores(qi) for qi in range(min(ATTN_LOOKAHEAD, n_q))]
    for qi in range(n_q):
        lo, hi = qi * tq, (qi + 1) * tq
        if qi + ATTN_LOOKAHEAD < n_q:
            pending.append(scores(qi + ATTN_LOOKAHEAD))
        s_d, s_o = pending.pop(0)
        m = jnp.max(s_d, axis=-1, keepdims=True)
        if s_o is not None:
            m = jnp.maximum(m, jnp.max(s_o, axis=-1, keepdims=True))
        acc = _dot(jnp.exp2(s_d - m).astype(BF16), v1_s[lo:hi, :])
        if s_o is not None:
            acc = acc + _dot(jnp.exp2(s_o - m).astype(BF16), v1_s[:lo, :])
        o_ref[lo:hi, :] = (acc[:, :MLA_V] / acc[:, MLA_V:]).astype(o_ref.dtype)


def _mla_attn(q, k, v):
    b, nh, s, _ = q.shape
    tq = min(256, s)
    spec = lambda w: pl.BlockSpec((1, 1, s, w), lambda bi, h: (bi, h, 0, 0))
    return pl.pallas_call(
        functools.partial(_attn_kernel, tq=tq),
        grid=(b, nh),
        in_specs=[spec(MLA_QK), spec(MLA_QK), spec(MLA_V)],
        out_specs=pl.BlockSpec((s, MLA_V), lambda bi, h: (bi, h)),
        out_shape=jax.ShapeDtypeStruct((b * s, MLA_WIDTH), BF16),
        scratch_shapes=[pltpu.VMEM((s, 2 * MLA_V), BF16)],
        compiler_params=_cparams("parallel", "parallel"),
        name="mla_attn",
    )(q, k, v)


def _merge_kernel(x_ref, oa_ref, ob_ref, ga_ref, gb_ref, wa_ref, wb_ref, wo_ref, o_ref):
    ya = jax.nn.sigmoid(ga_ref[...].astype(F32)) * _dot(oa_ref[...], wa_ref[...])
    yb = jax.nn.sigmoid(gb_ref[...].astype(F32)) * _dot(ob_ref[...], wb_ref[...])
    o_ref[...] = x_ref[...] + _dot((ya + yb).astype(BF16), wo_ref[...])


def _merge(x2, o_a, o_b, proj, w_a, w_b, w_o):
    t = x2.shape[0]
    tm = min(512, t)
    row = lambda w, blk: pl.BlockSpec((tm, w), lambda i: (i, blk))
    return pl.pallas_call(
        _merge_kernel,
        grid=(t // tm,),
        in_specs=[
            row(D_MODEL, 0), row(DN_WIDTH, 0), row(MLA_WIDTH, 0), row(D_MODEL, GATE_BLK0), row(D_MODEL, GATE_BLK0 + 1),
            _resident((DN_WIDTH, D_MODEL)), _resident((MLA_WIDTH, D_MODEL)), _resident((D_MODEL, D_MODEL)),
        ],
        out_specs=row(D_MODEL, 0),
        out_shape=jax.ShapeDtypeStruct((t, D_MODEL), F32),
        compiler_params=_cparams("parallel"),
        name="merge",
    )(x2, o_a, o_b, proj, proj, w_a, w_b, w_o)


def _mlp_kernel(x_ref, gm_ref, up_ref, dn_ref, o_ref, hm_ref):
    @pl.when(pl.program_id(1) == 0)
    def _():
        x = x_ref[...]
        hm_ref[...] = _rms(x, gm_ref[...]).astype(BF16)
        o_ref[...] = x

    a = _dot(hm_ref[...], up_ref[...])
    o_ref[...] += _dot(jnp.square(jnp.maximum(a, 0.0)).astype(BF16), dn_ref[...])


def _mlp(x1, gm, w_up, w_dn):
    t = x1.shape[0]
    tm = min(1024, t)
    tf = 1024
    return pl.pallas_call(
        _mlp_kernel,
        grid=(t // tm, D_FF // tf),
        in_specs=[
            pl.BlockSpec((tm, D_MODEL), lambda i, f: (i, 0)),
            _resident((1, D_MODEL)),
            pl.BlockSpec((D_MODEL, tf), lambda i, f: (0, f)),
            pl.BlockSpec((tf, D_MODEL), lambda i, f: (f, 0)),
        ],
        out_specs=pl.BlockSpec((tm, D_MODEL), lambda i, f: (i, 0)),
        out_shape=jax.ShapeDtypeStruct((t, D_MODEL), F32),
        scratch_shapes=[pltpu.VMEM((tm, D_MODEL), BF16)],
        compiler_params=_cparams("parallel", "arbitrary"),
        name="mlp",
    )(x1, gm, w_up, w_dn)


def _ple_kernel(x_ref, gp_ref, wg_ref, p_ref, wp_ref, o_ref):
    x = x_ref[...]
    gate = jax.nn.sigmoid(_dot(_rms(x, gp_ref[...]).astype(BF16), wg_ref[...]))
    o_ref[...] = x + gate * _dot(p_ref[...].astype(BF16), wp_ref[...])


def _ple(x2, gp, w_gate, p2, w_ple):
    t = x2.shape[0]
    tm = min(512, t)
    row = lambda w: pl.BlockSpec((tm, w), lambda i: (i, 0))
    return pl.pallas_call(
        _ple_kernel,
        grid=(t // tm,),
        in_specs=[row(D_MODEL), _resident((1, D_MODEL)), _resident((D_MODEL, D_MODEL)), row(PLE_DIM),
                  _resident((PLE_DIM, D_MODEL))],
        out_specs=row(D_MODEL),
        out_shape=jax.ShapeDtypeStruct((t, D_MODEL), F32),
        compiler_params=_cparams("parallel"),
        name="ple",
    )(x2, gp, w_gate, p2, w_ple)


def _regroup_w_in(w):
    d = w.shape[0]
    mq = w[:, _OFF_MQ:_OFF_CKV].reshape(d, MLA_HEADS, MLA_QK)
    main = jnp.concatenate([
        mq[:, :, :MLA_NOPE].reshape(d, MLA_HEADS * MLA_NOPE),
        mq[:, :, MLA_NOPE:].reshape(d, MLA_HEADS * MLA_ROPE),
        w[:, _OFF_CKV:_OFF_KR],
        w[:, _OFF_Q:_OFF_BETA],
        w[:, _OFF_GA:D_IN],
    ], axis=1).astype(BF16)
    tail = jnp.concatenate([
        w[:, _OFF_KR:_OFF_GA],
        w[:, _OFF_BETA:_OFF_MQ],
        jnp.zeros((d, LANES - MLA_ROPE - 2 * DN_HEADS), w.dtype),
    ], axis=1).astype(BF16)
    return main, tail


def kernel(x, p, positions, mix_norm, w_in, conv_w, dt_bias, a_log, dn_out_norm, ckv_norm, w_kv_up, q_nope_norm,
           q_rope_norm, k_nope_norm, k_rope_norm, w_branch_a, w_branch_b, w_out, mlp_norm, w_mlp_up, w_mlp_down,
           ple_norm, w_ple_gate, w_ple):
    b, s, d = x.shape
    t = b * s
    depth = w_in.shape[0]
    half = MLA_ROPE // 2
    inv = ROPE_BASE ** (-jnp.arange(half, dtype=F32) / half)
    inv64 = jnp.concatenate([inv, inv]).reshape(1, MLA_ROPE)
    pos = positions.reshape(t, 1)
    row = lambda a: a.reshape(1, -1).astype(F32)

    x2 = x.reshape(t, d)
    for i in range(depth):
        w_main, w_tail = _regroup_w_in(w_in[i])
        proj, tail = _in_proj(x2, row(mix_norm[i]), w_main, w_tail)
        o_a = _gdn(proj, tail, conv_w[i], a_log[i], dt_bias[i], row(dn_out_norm[i]), b, s)
        q_b, k_b, v_b = _mla_prep(proj, tail, pos, inv64, row(q_nope_norm[i]), row(q_rope_norm[i]),
                                  row(ckv_norm[i]), row(k_nope_norm[i]), row(k_rope_norm[i]),
                                  w_kv_up[i].astype(BF16), b, s)
        o_b = _mla_attn(q_b, k_b, v_b)
        x1 = _merge(x2, o_a, o_b, proj, w_branch_a[i].astype(BF16), w_branch_b[i].astype(BF16),
                    w_out[i].astype(BF16))
        x2 = _mlp(x1, row(mlp_norm[i]), w_mlp_up[i].astype(BF16), w_mlp_down[i].astype(BF16))
        x2 = _ple(x2, row(ple_norm[i]), w_ple_gate[i].astype(BF16), p[i].reshape(t, PLE_DIM), w_ple[i].astype(BF16))
    return x2.reshape(b, s, d)
```

```python
import functools

import jax
import jax.numpy as jnp
from jax import lax
from jax.experimental import pallas as pl
from jax.experimental.pallas import tpu as pltpu

F32 = jnp.float32
BF16 = jnp.bfloat16

D_MODEL = 2048
CHUNK = 64
PLE_DIM = 256
EPS = 1e-6
DN_HEADS = 8
DN_HEAD_DIM = 128
DN_WIDTH = DN_HEADS * DN_HEAD_DIM
CONV_K = 4
MLA_HEADS = 8
MLA_NOPE = 128
MLA_ROPE = 64
MLA_QK = MLA_NOPE + MLA_ROPE
MLA_V = 128
KV_RANK = 512
MLA_WIDTH = MLA_HEADS * MLA_V
ROPE_BASE = 10000.0
D_FF = 4 * D_MODEL
LOG2_E = 1.4426950408889634
HALF_PI = 1.5707963267948966

_OFF_Q = 0
_OFF_K = _OFF_Q + DN_WIDTH
_OFF_V = _OFF_K + DN_WIDTH
_OFF_Z = _OFF_V + DN_WIDTH
_OFF_BETA = _OFF_Z + DN_WIDTH
_OFF_ALPHA = _OFF_BETA + DN_HEADS
_OFF_MQ = _OFF_ALPHA + DN_HEADS
_OFF_CKV = _OFF_MQ + MLA_HEADS * MLA_QK
_OFF_KR = _OFF_CKV + KV_RANK
_OFF_GA = _OFF_KR + MLA_ROPE
_OFF_GB = _OFF_GA + D_MODEL
D_IN = _OFF_GB + D_MODEL

MQ_W = MLA_HEADS * MLA_QK
CKV_BLK = MQ_W // KV_RANK
GATE_BLK0 = (MQ_W + KV_RANK + 4 * DN_WIDTH) // D_MODEL
N_MAIN = MQ_W + KV_RANK + 4 * DN_WIDTH + 2 * D_MODEL
LANES = 128
TAIL_BETA = MLA_ROPE
TAIL_ALPHA = MLA_ROPE + DN_HEADS

VMEM_LIMIT = 60 * 1024 * 1024


def _cparams(*sem):
    return pltpu.CompilerParams(dimension_semantics=sem, vmem_limit_bytes=VMEM_LIMIT)


def _rms(t, g):
    return t * lax.rsqrt(jnp.mean(t * t, axis=-1, keepdims=True) + EPS) * g


def _dot(a, b):
    return jnp.dot(a, b, preferred_element_type=F32)


def _dot_nt(a, b):
    return lax.dot_general(a, b, (((1,), (1,)), ((), ())), preferred_element_type=F32)


def _dot_tn(a, b):
    return lax.dot_general(a, b, (((0,), (0,)), ((), ())), preferred_element_type=F32)


def _resident(shape):
    nd = len(shape)
    return pl.BlockSpec(shape, lambda *_: (0,) * nd, pipeline_mode=pl.Buffered(1))


IN_PROJ_TN = 2048
assert MQ_W + KV_RANK == IN_PROJ_TN


def _inproj_kernel(x_ref, g_ref, w_ref, wt_ref, cw_ref, pos_ref, inv_ref, qnn_ref, qrn_ref, cn_ref, knn_ref, krn_ref,
                   wkv_ref,
                   o_ref, t_ref, q_out, k_out, v_out, qkv_out,
                   h_s, mla_s, trig_s, qk_s, v_s, pad_s, halo_s, *, tiles_per_seq):
    i = pl.program_id(0)
    j = pl.program_id(1)
    half = MLA_ROPE // 2
    scale = MLA_QK ** -0.5 * LOG2_E
    keep_halo = jnp.where(i % tiles_per_seq == 0, 0.0, 1.0).astype(F32)

    def project():
        acc = _dot(h_s[...], w_ref[...])
        o_ref[...] = acc.astype(o_ref.dtype)
        return acc

    def rope(t):
        cos = trig_s[:, :MLA_ROPE]
        sin = trig_s[:, MLA_ROPE:]
        rot = jnp.concatenate([-t[:, half:], t[:, :half]], axis=1)
        return t * cos + rot * sin

    @pl.when(j == 0)
    def _():
        h = _rms(x_ref[...], g_ref[...]).astype(BF16)
        h_s[...] = h
        t_ref[...] = _dot(h, wt_ref[...])
        ang = pos_ref[...].astype(F32) * inv_ref[...]
        trig_s[...] = jnp.cos(jnp.where(lax.broadcasted_iota(jnp.int32, ang.shape, 1) < MLA_ROPE, ang, ang - HALF_PI))
        project()
        mla_s[...] = o_ref[...]

        @pl.when(i == 0)
        def _():
            halo_s[...] = jnp.zeros(halo_s.shape, F32)

    def conv_silu(src_s, c0, qkv_col, slot):
        cols = slice(qkv_col, qkv_col + DN_HEAD_DIM)
        w = cw_ref[:, cols]
        tm = src_s.shape[0]
        pad_s[slot, :CONV_PAD, :] = halo_s[:, cols] * keep_halo
        pad_s[slot, CONV_PAD:, :] = src_s[:, c0:c0 + DN_HEAD_DIM].astype(F32)
        halo_s[:, cols] = pad_s[slot, tm:tm + CONV_PAD, :]
        acc = pad_s[slot, CONV_PAD:, :] * w[CONV_K - 1:CONV_K, :]
        for d in range(1, CONV_K):
            acc = acc + pad_s[slot, CONV_PAD - d:CONV_PAD - d + tm, :] * w[CONV_K - 1 - d:CONV_K - d, :]
        return acc * jax.nn.sigmoid(acc)

    def l2norm(t):
        return t * lax.rsqrt(jnp.sum(t * t, axis=-1, keepdims=True) + EPS)

    @pl.when(j == 1)
    def _():
        project()
        qk_s[...] = o_ref[...]
        rope0 = MLA_HEADS * MLA_NOPE
        for h in range(MLA_HEADS):
            qn = mla_s[:, h * MLA_NOPE:(h + 1) * MLA_NOPE].astype(F32)
            qr = mla_s[:, rope0 + h * MLA_ROPE:rope0 + (h + 1) * MLA_ROPE].astype(F32)
            q_out[0, h, :, :MLA_NOPE] = (_rms(qn, qnn_ref[...]) * scale).astype(BF16)
            q_out[0, h, :, MLA_NOPE:] = (rope(_rms(qr, qrn_ref[...])) * scale).astype(BF16)

    @pl.when(j == 2)
    def _():
        project()
        v_s[...] = o_ref[:, :DN_WIDTH]
        kr = rope(_rms(t_ref[...][:, :MLA_ROPE], krn_ref[...])).astype(BF16)
        c = _rms(mla_s[:, MQ_W:].astype(F32), cn_ref[...]).astype(BF16)
        kv = _dot(c, wkv_ref[...])
        for h in range(MLA_HEADS):
            kv0 = h * (MLA_NOPE + MLA_V)
            k_out[0, h, :, :MLA_NOPE] = _rms(kv[:, kv0:kv0 + MLA_NOPE], knn_ref[...]).astype(BF16)
            k_out[0, h, :, MLA_NOPE:] = kr
            v_out[0, h, :, :] = kv[:, kv0 + MLA_NOPE:kv0 + MLA_NOPE + MLA_V].astype(BF16)

    @pl.when(j == 3)
    def _():
        project()
        for n in range(2 * DN_HEADS):
            c0 = n * DN_HEAD_DIM
            qkv_out[:, c0:c0 + DN_HEAD_DIM] = l2norm(conv_silu(qk_s, c0, c0, n % 2)).astype(BF16)

    @pl.when(j == 4)
    def _():
        project()
        for n in range(DN_HEADS):
            c0 = n * DN_HEAD_DIM
            col = 2 * DN_WIDTH + c0
            qkv_out[:, col:col + DN_HEAD_DIM] = conv_silu(v_s, c0, col, n % 2).astype(BF16)


def _in_proj(x2, gain, w_main, w_tail, conv_w, pos, inv128, qnn, qrn, cn, knn, krn, wkv, b, s):
    t = x2.shape[0]
    tm = min(512, s)
    tn = IN_PROJ_TN
    nsb = s // tm
    nh = MLA_HEADS
    head_spec = lambda w: pl.BlockSpec((1, nh, tm, w), lambda i, j: (i // nsb, 0, i % nsb, 0))
    assert N_MAIN // tn == 5
    return pl.pallas_call(
        functools.partial(_inproj_kernel, tiles_per_seq=nsb),
        grid=(t // tm, N_MAIN // tn),
        in_specs=[
            pl.BlockSpec((tm, D_MODEL), lambda i, j: (i, 0)),
            _resident((1, D_MODEL)),
            pl.BlockSpec((D_MODEL, tn), lambda i, j: (0, j)),
            _resident((D_MODEL, LANES)),
            _resident((CONV_K, 3 * DN_WIDTH)),
            pl.BlockSpec((tm, 1), lambda i, j: (i, 0)),
            _resident((1, 2 * MLA_ROPE)), _resident((1, MLA_NOPE)), _resident((1, MLA_ROPE)),
            _resident((1, KV_RANK)), _resident((1, MLA_NOPE)), _resident((1, MLA_ROPE)),
            _resident((KV_RANK, nh * (MLA_NOPE + MLA_V))),
        ],
        out_specs=[
            pl.BlockSpec((tm, tn), lambda i, j: (i, j)),
            pl.BlockSpec((tm, LANES), lambda i, j: (i, 0)),
            head_spec(MLA_QK), head_spec(MLA_QK), head_spec(MLA_V),
            pl.BlockSpec((tm, 3 * DN_WIDTH), lambda i, j: (i, 0)),
        ],
        out_shape=[
            jax.ShapeDtypeStruct((t, N_MAIN), BF16),
            jax.ShapeDtypeStruct((t, LANES), F32),
            jax.ShapeDtypeStruct((b, nh, s, MLA_QK), BF16),
            jax.ShapeDtypeStruct((b, nh, s, MLA_QK), BF16),
            jax.ShapeDtypeStruct((b, nh, s, MLA_V), BF16),
            jax.ShapeDtypeStruct((t, 3 * DN_WIDTH), BF16),
        ],
        scratch_shapes=[
            pltpu.VMEM((tm, D_MODEL), BF16),
            pltpu.VMEM((tm, tn), BF16),
            pltpu.VMEM((tm, 2 * MLA_ROPE), F32),
            pltpu.VMEM((tm, 2 * DN_WIDTH), BF16),
            pltpu.VMEM((tm, DN_WIDTH), BF16),
            pltpu.VMEM((2, CONV_PAD + tm, DN_HEAD_DIM), F32),
            pltpu.VMEM((CONV_PAD, 3 * DN_WIDTH), F32),
        ],
        compiler_params=_cparams("arbitrary", "arbitrary"),
        name="in_proj",
    )(x2, gain, w_main, w_tail, conv_w, pos, inv128, qnn, qrn, cn, knn, krn, wkv)


GDN_GROUP = 4
GDN_PREP_CHUNKS = 4
CONV_PAD = 8


def _dot_r(x, m):
    return _dot(x.astype(BF16), m.astype(BF16))


def _gdn_kernel(alog_ref, dtb_ref, q_ref, k_ref, v_ref, z_ref, t_ref, on_ref,
                o_ref,
                g_s, k_s, kq_s, kv_s, wq_s, kd_s, u_s, qk_s, st_s):
    s = q_ref.shape[0]
    n_chunks = s // CHUNK
    hd = DN_HEAD_DIM
    group = pl.program_id(1)
    pos = lax.broadcasted_iota(jnp.int32, (s, 1), 0) % CHUNK
    lane = lax.broadcasted_iota(jnp.int32, (1, LANES), 1)

    def chunked(t):
        return t.reshape(n_chunks, CHUNK, t.shape[-1])

    def column(t, c):
        return jnp.sum(jnp.where(lane == c, t, 0.0), axis=1, keepdims=True)

    tail = t_ref[...]
    beta_all = jax.nn.sigmoid(tail)
    xs = tail + dtb_ref[...]
    softplus = jnp.maximum(xs, 0.0) + jnp.log1p(jnp.exp(-jnp.abs(xs)))
    gc_all = -jnp.exp(alog_ref[...]) * softplus
    step = 1
    while step < CHUNK:
        gc_all = gc_all + jnp.where(pos >= step, pltpu.roll(gc_all, step, 0), 0.0)
        step *= 2

    for hh in range(GDN_GROUP):
        head = group * GDN_GROUP + hh
        c0 = hh * hd
        qn = q_ref[:, c0:c0 + hd].astype(F32)
        kn = k_ref[:, c0:c0 + hd].astype(F32)
        vv = v_ref[:, c0:c0 + hd].astype(F32)

        beta = column(beta_all, TAIL_BETA + head)
        gc = column(gc_all, TAIL_ALPHA + head)
        gc3 = chunked(gc)
        g_last = jnp.broadcast_to(gc3[:, CHUNK - 1:CHUNK, :], gc3.shape).reshape(s, 1)
        eg = jnp.exp(gc)

        qs = qn * (hd ** -0.5)
        kb = kn * beta
        g_s[hh] = jnp.broadcast_to(gc, (s, LANES))
        k_s[hh] = kn.astype(BF16)
        kq_s[hh, :, :CHUNK, :] = chunked(kb.astype(BF16))
        kq_s[hh, :, CHUNK:, :] = chunked(qs.astype(BF16))
        kv_s[hh, :, :hd] = (kb * eg).astype(BF16)
        kv_s[hh, :, hd:] = (vv * beta).astype(BF16)
        wq_s[hh, :, CHUNK:, :] = chunked((qs * eg).astype(BF16))
        kd_s[hh] = (kn * jnp.exp(g_last - gc)).astype(BF16)
        st_s[hh] = jnp.zeros((hd, hd), F32)

    ri = lax.broadcasted_iota(jnp.int32, (CHUNK, CHUNK), 0)
    ci = lax.broadcasted_iota(jnp.int32, (CHUNK, CHUNK), 1)
    eye = ri == ci
    eye_f = jnp.where(eye, 1.0, 0.0).astype(F32)
    n_factors = CHUNK.bit_length() - 2

    heads = range(GDN_GROUP)

    def prep(it):
        items = [(hh, it * GDN_PREP_CHUNKS + c) for c in range(GDN_PREP_CHUNKS) for hh in heads]
        sls = [pl.ds(pl.multiple_of(n * CHUNK, CHUNK), CHUNK) for _, n in items]
        rs = [_dot_nt(kq_s[hh, n], k_s[hh, sl, :]) for (hh, n), sl in zip(items, sls)]
        yield
        ms, ps = [], []
        for (hh, n), sl, r in zip(items, sls, rs):
            g_col = g_s[hh, sl, :][:, :CHUNK]
            g_row = jnp.sum(jnp.where(eye, g_col, 0.0), axis=0, keepdims=True)
            decay = jnp.exp(jnp.where(ri >= ci, g_col - g_row, -jnp.inf))
            a = jnp.where(ri > ci, r[:CHUNK] * decay, 0.0)
            qk_s[hh, sl, :] = (r[CHUNK:] * decay).astype(BF16)
            ms.append(a)
            ps.append(eye_f - a)
        ms = [_dot_r(a, a) for a in ms]
        yield
        for _ in range(n_factors - 1):
            rs = [_dot_r(jnp.concatenate([m, p], axis=0), m) for m, p in zip(ms, ps)]
            ms = [r[:CHUNK] for r in rs]
            ps = [p + r[CHUNK:] for p, r in zip(ps, rs)]
            yield
        rs = [_dot_r(p, m) for m, p in zip(ms, ps)]
        yield
        wus = [_dot((p + r).astype(BF16), kv_s[hh, sl, :]) for (hh, _), sl, p, r in zip(items, sls, ps, rs)]
        for (hh, n), sl, wu in zip(items, sls, wus):
            wq_s[hh, n, :CHUNK, :] = wu[:, :hd].astype(BF16)
            u_s[hh, sl, :] = wu[:, hd:]
        yield

    def scan(n):
        r0 = pl.multiple_of(n * CHUNK, CHUNK)
        sl = pl.ds(r0, CHUNK)
        states = [st_s[hh] for hh in heads]
        rs = [_dot(wq_s[hh, n], states[hh].astype(BF16)) for hh in heads]
        yield
        v_nbs = [(u_s[hh, sl, :] - rs[hh][:CHUNK]).astype(BF16) for hh in heads]
        os = [rs[hh][CHUNK:] + _dot(qk_s[hh, sl, :], v_nbs[hh]) for hh in heads]
        kvs = [_dot_tn(kd_s[hh, sl, :], v_nbs[hh]) for hh in heads]
        yield
        for hh in heads:
            decay_last = jnp.exp(g_s[hh, pl.ds(r0 + CHUNK - 1, 1), :])
            st_s[hh] = states[hh] * decay_last + kvs[hh]
            z = z_ref[sl, hh * hd:(hh + 1) * hd].astype(F32)
            o_ref[sl, hh * hd:(hh + 1) * hd] = (_rms(os[hh], on_ref[...]) * (z * jax.nn.sigmoid(z))).astype(o_ref.dtype)

    n_groups = n_chunks // GDN_PREP_CHUNKS
    for _ in prep(0):
        pass

    def body(it, carry):
        prep_stages = prep((it + 1) % n_groups)
        for c in range(GDN_PREP_CHUNKS):
            for _ in scan(it * GDN_PREP_CHUNKS + c):
                next(prep_stages, None)
        for _ in prep_stages:
            pass
        return carry

    lax.fori_loop(0, n_groups, body, 0)


def _gdn(qkv, proj, tail, a_log, dt_bias, out_norm, b, s):
    hd = DN_HEAD_DIM
    gw = hd * GDN_GROUP
    n_groups = DN_HEADS // GDN_GROUP
    n_chunks = s // CHUNK
    z_blk0 = (MQ_W + KV_RANK + 3 * DN_WIDTH) // gw
    col = lambda k: pl.BlockSpec((s, gw), lambda bi, g: (bi, k * n_groups + g))
    per_head = lambda shape, dt: pltpu.VMEM((GDN_GROUP,) + shape, dt)
    lane_row = lambda a: jnp.zeros((1, LANES), F32).at[0, TAIL_ALPHA:TAIL_ALPHA + DN_HEADS].set(a.astype(F32))
    return pl.pallas_call(
        _gdn_kernel,
        grid=(b, n_groups),
        in_specs=[
            _resident((1, LANES)), _resident((1, LANES)),
            col(0), col(1), col(2),
            pl.BlockSpec((s, gw), lambda bi, g: (bi, z_blk0 + g)),
            pl.BlockSpec((s, LANES), lambda bi, g: (bi, 0)),
            _resident((1, hd)),
        ],
        out_specs=pl.BlockSpec((s, gw), lambda bi, g: (bi, g)),
        out_shape=jax.ShapeDtypeStruct((b * s, DN_WIDTH), BF16),
        scratch_shapes=[
            per_head((s, LANES), F32),
            per_head((s, hd), BF16),
            per_head((n_chunks, 2 * CHUNK, hd), BF16),
            per_head((s, 2 * hd), BF16),
            per_head((n_chunks, 2 * CHUNK, hd), BF16),
            per_head((s, hd), BF16),
            per_head((s, hd), F32),
            per_head((s, CHUNK), BF16),
            per_head((hd, hd), F32),
        ],
        compiler_params=_cparams("parallel", "arbitrary"),
        name="gdn",
    )(lane_row(a_log), lane_row(dt_bias), qkv, qkv, qkv, proj, tail, out_norm)


ATTN_LOOKAHEAD = 1


def _attn_kernel(q_ref, k_ref, v_ref, o_ref, v1_s, *, tq):
    s = q_ref.shape[2]
    v1_s[:, :MLA_V] = v_ref[0, 0]
    v1_s[:, MLA_V:] = jnp.ones((s, MLA_V), BF16)
    ri = lax.broadcasted_iota(jnp.int32, (tq, tq), 0) // CHUNK
    ci = lax.broadcasted_iota(jnp.int32, (tq, tq), 1) // CHUNK
    allowed = ci <= ri
    n_q = s // tq

    def scores(qi):
        lo, hi = qi * tq, (qi + 1) * tq
        q = q_ref[0, 0, lo:hi, :]
        s_d = jnp.where(allowed, _dot_nt(q, k_ref[0, 0, lo:hi, :]), -jnp.inf)
        s_o = _dot_nt(q, k_ref[0, 0, :lo, :]) if qi > 0 else None
        return s_d, s_o

    pending = [scores(qi) for qi in range(min(ATTN_LOOKAHEAD, n_q))]
    for qi in range(n_q):
        lo, hi = qi * tq, (qi + 1) * tq
        if qi + ATTN_LOOKAHEAD < n_q:
            pending.append(scores(qi + ATTN_LOOKAHEAD))
        s_d, s_o = pending.pop(0)
        m = jnp.max(s_d, axis=-1, keepdims=True)
        if s_o is not None:
            m = jnp.maximum(m, jnp.max(s_o, axis=-1, keepdims=True))
        acc = _dot(jnp.exp2(s_d - m).astype(BF16), v1_s[lo:hi, :])
        if s_o is not None:
            acc = acc + _dot(jnp.exp2(s_o - m).astype(BF16), v1_s[:lo, :])
        o_ref[lo:hi, :] = (acc[:, :MLA_V] / acc[:, MLA_V:]).astype(o_ref.dtype)


def _mla_attn(q, k, v):
    b, nh, s, _ = q.shape
    tq = min(256, s)
    spec = lambda w: pl.BlockSpec((1, 1, s, w), lambda bi, h: (bi, h, 0, 0))
    return pl.pallas_call(
        functools.partial(_attn_kernel, tq=tq),
        grid=(b, nh),
        in_specs=[spec(MLA_QK), spec(MLA_QK), spec(MLA_V)],
        out_specs=pl.BlockSpec((s, MLA_V), lambda bi, h: (bi, h)),
        out_shape=jax.ShapeDtypeStruct((b * s, MLA_WIDTH), BF16),
        scratch_shapes=[pltpu.VMEM((s, 2 * MLA_V), BF16)],
        compiler_params=_cparams("parallel", "parallel"),
        name="mla_attn",
    )(q, k, v)


def _merge_kernel(x_ref, oa_ref, ob_ref, ga_ref, gb_ref, wa_ref, wb_ref, wo_ref, o_ref):
    ya = jax.nn.sigmoid(ga_ref[...].astype(F32)) * _dot(oa_ref[...], wa_ref[...])
    yb = jax.nn.sigmoid(gb_ref[...].astype(F32)) * _dot(ob_ref[...], wb_ref[...])
    o_ref[...] = x_ref[...] + _dot((ya + yb).astype(BF16), wo_ref[...])


def _merge(x2, o_a, o_b, proj, w_a, w_b, w_o):
    t = x2.shape[0]
    tm = min(512, t)
    row = lambda w, blk: pl.BlockSpec((tm, w), lambda i: (i, blk))
    return pl.pallas_call(
        _merge_kernel,
        grid=(t // tm,),
        in_specs=[
            row(D_MODEL, 0), row(DN_WIDTH, 0), row(MLA_WIDTH, 0), row(D_MODEL, GATE_BLK0), row(D_MODEL, GATE_BLK0 + 1),
            _resident((DN_WIDTH, D_MODEL)), _resident((MLA_WIDTH, D_MODEL)), _resident((D_MODEL, D_MODEL)),
        ],
        out_specs=row(D_MODEL, 0),
        out_shape=jax.ShapeDtypeStruct((t, D_MODEL), F32),
        compiler_params=_cparams("parallel"),
        name="merge",
    )(x2, o_a, o_b, proj, proj, w_a, w_b, w_o)


def _mlp_kernel(x_ref, gm_ref, up_ref, dn_ref, o_ref, hm_ref):
    @pl.when(pl.program_id(1) == 0)
    def _():
        x = x_ref[...]
        hm_ref[...] = _rms(x, gm_ref[...]).astype(BF16)
        o_ref[...] = x

    a = _dot(hm_ref[...], up_ref[...])
    o_ref[...] += _dot(jnp.square(jnp.maximum(a, 0.0)).astype(BF16), dn_ref[...])


def _mlp(x1, gm, w_up, w_dn):
    t = x1.shape[0]
    tm = min(1024, t)
    tf = 1024
    return pl.pallas_call(
        _mlp_kernel,
        grid=(t // tm, D_FF // tf),
        in_specs=[
            pl.BlockSpec((tm, D_MODEL), lambda i, f: (i, 0)),
            _resident((1, D_MODEL)),
            pl.BlockSpec((D_MODEL, tf), lambda i, f: (0, f)),
            pl.BlockSpec((tf, D_MODEL), lambda i, f: (f, 0)),
        ],
        out_specs=pl.BlockSpec((tm, D_MODEL), lambda i, f: (i, 0)),
        out_shape=jax.ShapeDtypeStruct((t, D_MODEL), F32),
        scratch_shapes=[pltpu.VMEM((tm, D_MODEL), BF16)],
        compiler_params=_cparams("parallel", "arbitrary"),
        name="mlp",
    )(x1, gm, w_up, w_dn)


def _ple_kernel(x_ref, gp_ref, wg_ref, p_ref, wp_ref, o_ref):
    x = x_ref[...]
    gate = jax.nn.sigmoid(_dot(_rms(x, gp_ref[...]).astype(BF16), wg_ref[...]))
    o_ref[...] = x + gate * _dot(p_ref[...].astype(BF16), wp_ref[...])


def _ple(x2, gp, w_gate, p2, w_ple):
    t = x2.shape[0]
    tm = min(512, t)
    row = lambda w: pl.BlockSpec((tm, w), lambda i: (i, 0))
    return pl.pallas_call(
        _ple_kernel,
        grid=(t // tm,),
        in_specs=[row(D_MODEL), _resident((1, D_MODEL)), _resident((D_MODEL, D_MODEL)), row(PLE_DIM),
                  _resident((PLE_DIM, D_MODEL))],
        out_specs=row(D_MODEL),
        out_shape=jax.ShapeDtypeStruct((t, D_MODEL), F32),
        compiler_params=_cparams("parallel"),
        name="ple",
    )(x2, gp, w_gate, p2, w_ple)


def _regroup_w_in(w):
    d = w.shape[0]
    mq = w[:, _OFF_MQ:_OFF_CKV].reshape(d, MLA_HEADS, MLA_QK)
    main = jnp.concatenate([
        mq[:, :, :MLA_NOPE].reshape(d, MLA_HEADS * MLA_NOPE),
        mq[:, :, MLA_NOPE:].reshape(d, MLA_HEADS * MLA_ROPE),
        w[:, _OFF_CKV:_OFF_KR],
        w[:, _OFF_Q:_OFF_BETA],
        w[:, _OFF_GA:D_IN],
    ], axis=1).astype(BF16)
    tail = jnp.concatenate([
        w[:, _OFF_KR:_OFF_GA],
        w[:, _OFF_BETA:_OFF_MQ],
        jnp.zeros((d, LANES - MLA_ROPE - 2 * DN_HEADS), w.dtype),
    ], axis=1).astype(BF16)
    return main, tail


def kernel(x, p, positions, mix_norm, w_in, conv_w, dt_bias, a_log, dn_out_norm, ckv_norm, w_kv_up, q_nope_norm,
           q_rope_norm, k_nope_norm, k_rope_norm, w_branch_a, w_branch_b, w_out, mlp_norm, w_mlp_up, w_mlp_down,
           ple_norm, w_ple_gate, w_ple):
    b, s, d = x.shape
    t = b * s
    depth = w_in.shape[0]
    half = MLA_ROPE // 2
    inv = ROPE_BASE ** (-jnp.arange(half, dtype=F32) / half)
    inv64 = jnp.concatenate([inv, inv, inv, inv]).reshape(1, 2 * MLA_ROPE)
    pos = positions.reshape(t, 1)
    row = lambda a: a.reshape(1, -1).astype(F32)

    x2 = x.reshape(t, d)
    for i in range(depth):
        w_main, w_tail = _regroup_w_in(w_in[i])
        proj, tail, q_b, k_b, v_b, qkv = _in_proj(
            x2, row(mix_norm[i]), w_main, w_tail, conv_w[i], pos, inv64, row(q_nope_norm[i]), row(q_rope_norm[i]),
            row(ckv_norm[i]), row(k_nope_norm[i]), row(k_rope_norm[i]), w_kv_up[i].astype(BF16), b, s)
        o_a = _gdn(qkv, proj, tail, a_log[i], dt_bias[i], row(dn_out_norm[i]), b, s)
        o_b = _mla_attn(q_b, k_b, v_b)
        x1 = _merge(x2, o_a, o_b, proj, w_branch_a[i].astype(BF16), w_branch_b[i].astype(BF16),
                    w_out[i].astype(BF16))
        x2 = _mlp(x1, row(mlp_norm[i]), w_mlp_up[i].astype(BF16), w_mlp_down[i].astype(BF16))
        x2 = _ple(x2, row(ple_norm[i]), w_ple_gate[i].astype(BF16), p[i].reshape(t, PLE_DIM), w_ple[i].astype(BF16))
    return x2.reshape(b, s, d)
```

```python
import functools

import jax
import jax.numpy as jnp
from jax import lax
from jax.experimental import pallas as pl
from jax.experimental.pallas import tpu as pltpu

F32 = jnp.float32
BF16 = jnp.bfloat16

D_MODEL = 2048
CHUNK = 64
PLE_DIM = 256
EPS = 1e-6
DN_HEADS = 8
DN_HEAD_DIM = 128
DN_WIDTH = DN_HEADS * DN_HEAD_DIM
CONV_K = 4
MLA_HEADS = 8
MLA_NOPE = 128
MLA_ROPE = 64
MLA_QK = MLA_NOPE + MLA_ROPE
MLA_V = 128
KV_RANK = 512
MLA_WIDTH = MLA_HEADS * MLA_V
ROPE_BASE = 10000.0
D_FF = 4 * D_MODEL
LOG2_E = 1.4426950408889634
HALF_PI = 1.5707963267948966

_OFF_Q = 0
_OFF_K = _OFF_Q + DN_WIDTH
_OFF_V = _OFF_K + DN_WIDTH
_OFF_Z = _OFF_V + DN_WIDTH
_OFF_BETA = _OFF_Z + DN_WIDTH
_OFF_ALPHA = _OFF_BETA + DN_HEADS
_OFF_MQ = _OFF_ALPHA + DN_HEADS
_OFF_CKV = _OFF_MQ + MLA_HEADS * MLA_QK
_OFF_KR = _OFF_CKV + KV_RANK
_OFF_GA = _OFF_KR + MLA_ROPE
_OFF_GB = _OFF_GA + D_MODEL
D_IN = _OFF_GB + D_MODEL

MQ_W = MLA_HEADS * MLA_QK
CKV_BLK = MQ_W // KV_RANK
LANES = 128
TAIL_BETA = MLA_ROPE
TAIL_ALPHA = MLA_ROPE + DN_HEADS

VMEM_LIMIT = 60 * 1024 * 1024


def _cparams(*sem):
    return pltpu.CompilerParams(dimension_semantics=sem, vmem_limit_bytes=VMEM_LIMIT)


def _rms(t, g):
    return t * lax.rsqrt(jnp.mean(t * t, axis=-1, keepdims=True) + EPS) * g


def _dot(a, b):
    return jnp.dot(a, b, preferred_element_type=F32)


def _dot_nt(a, b):
    return lax.dot_general(a, b, (((1,), (1,)), ((), ())), preferred_element_type=F32)


def _dot_tn(a, b):
    return lax.dot_general(a, b, (((0,), (0,)), ((), ())), preferred_element_type=F32)


def _resident(shape):
    nd = len(shape)
    return pl.BlockSpec(shape, lambda *_: (0,) * nd, pipeline_mode=pl.Buffered(1))


IN_PROJ_TN = 2048
assert MQ_W + KV_RANK == IN_PROJ_TN and 2 * DN_WIDTH == IN_PROJ_TN and D_MODEL == IN_PROJ_TN
CONV_PAD = 8
CONV_ROWS = 128
PROJ0_ROWS = 256


def _l2norm(t):
    return t * lax.rsqrt(jnp.sum(t * t, axis=-1, keepdims=True) + EPS)


def _rope(t, trig):
    half = MLA_ROPE // 2
    rot = jnp.concatenate([-t[:, half:], t[:, :half]], axis=1)
    return t * trig[:, :MLA_ROPE] + rot * trig[:, MLA_ROPE:]


def _proj0_kernel(x_ref, g_ref, w_ref, wt_ref, pos_ref, inv_ref, h_out, o_ref, t_ref, trig_out):
    block = min(PROJ0_ROWS, x_ref.shape[0])
    for r0 in range(0, x_ref.shape[0], block):
        rows = slice(r0, r0 + block)
        h = _rms(x_ref[rows, :], g_ref[...]).astype(BF16)
        h_out[rows, :] = h
        t_ref[rows, :] = _dot(h, wt_ref[...])
        o_ref[rows, :] = _dot(h, w_ref[...]).astype(BF16)
    ang = pos_ref[...].astype(F32) * inv_ref[...]
    trig_out[...] = jnp.cos(jnp.where(lax.broadcasted_iota(jnp.int32, ang.shape, 1) < MLA_ROPE, ang, ang - HALF_PI))


def _proj1_kernel(h_ref, w_ref, mq_ref, trig_ref, qnn_ref, qrn_ref, o_ref, q_out):
    o_ref[...] = _dot(h_ref[...], w_ref[...]).astype(BF16)
    scale = MLA_QK ** -0.5 * LOG2_E
    trig = trig_ref[...]
    rope0 = MLA_HEADS * MLA_NOPE
    for h in range(MLA_HEADS):
        qn = mq_ref[:, h * MLA_NOPE:(h + 1) * MLA_NOPE].astype(F32)
        qr = mq_ref[:, rope0 + h * MLA_ROPE:rope0 + (h + 1) * MLA_ROPE].astype(F32)
        q_out[0, h, :, :MLA_NOPE] = (_rms(qn, qnn_ref[...]) * scale).astype(BF16)
        q_out[0, h, :, MLA_NOPE:] = (_rope(_rms(qr, qrn_ref[...]), trig) * scale).astype(BF16)


def _proj2_kernel(h_ref, w_ref, ckv_ref, t_ref, trig_ref, cn_ref, knn_ref, krn_ref, wkv_ref, o_ref, k_out, v_out):
    o_ref[...] = _dot(h_ref[...], w_ref[...]).astype(BF16)
    kr = _rope(_rms(t_ref[...][:, :MLA_ROPE], krn_ref[...]), trig_ref[...]).astype(BF16)
    c = _rms(ckv_ref[...].astype(F32), cn_ref[...]).astype(BF16)
    kv = _dot(c, wkv_ref[...])
    for h in range(MLA_HEADS):
        kv0 = h * (MLA_NOPE + MLA_V)
        k_out[0, h, :, :MLA_NOPE] = _rms(kv[:, kv0:kv0 + MLA_NOPE], knn_ref[...]).astype(BF16)
        k_out[0, h, :, MLA_NOPE:] = kr
        v_out[0, h, :, :] = kv[:, kv0 + MLA_NOPE:kv0 + MLA_NOPE + MLA_V].astype(BF16)


def _conv_silu_tile(src_ref, cw_ref, cw_col0, n_slices, pad_s, halo_s, tiles_per_seq, finish, dst_ref):
    tm = src_ref.shape[0]
    i = pl.program_id(0)

    @pl.when(i == 0)
    def _():
        halo_s[...] = jnp.zeros(halo_s.shape, F32)

    keep = jnp.where(i % tiles_per_seq == 0, 0.0, 1.0).astype(F32)
    for n in range(n_slices):
        yield
        cols = slice(n * DN_HEAD_DIM, (n + 1) * DN_HEAD_DIM)
        slot = n % pad_s.shape[0]
        w = cw_ref[:, cw_col0 + n * DN_HEAD_DIM:cw_col0 + (n + 1) * DN_HEAD_DIM]
        pad_s[slot, :CONV_PAD, :] = halo_s[:, cols] * keep
        pad_s[slot, CONV_PAD:, :] = src_ref[:, cols].astype(F32)
        halo_s[:, cols] = pad_s[slot, tm:tm + CONV_PAD, :]
        for r0 in range(0, tm, CONV_ROWS):
            acc = pad_s[slot, CONV_PAD + r0:CONV_PAD + r0 + CONV_ROWS, :] * w[CONV_K - 1:CONV_K, :]
            for d in range(1, CONV_K):
                lo = CONV_PAD + r0 - d
                acc = acc + pad_s[slot, lo:lo + CONV_ROWS, :] * w[CONV_K - 1 - d:CONV_K - d, :]
            dst_ref[r0:r0 + CONV_ROWS, cols] = finish(acc * jax.nn.sigmoid(acc)).astype(BF16)


PROJ_PARTS = 8


def _project_between(h_ref, w_ref, o_ref, stages, stages_per_part):
    width = o_ref.shape[1] // PROJ_PARTS
    next(stages)
    for part in range(PROJ_PARTS):
        cols = slice(part * width, (part + 1) * width)
        o_ref[:, cols] = _dot(h_ref[...], w_ref[:, cols]).astype(BF16)
        for _ in range(stages_per_part):
            next(stages, None)


def _proj3_kernel(h_ref, w_ref, qk_ref, cw_ref, o_ref, qk_out, pad_s, halo_s, *, tiles_per_seq):
    stages = _conv_silu_tile(qk_ref, cw_ref, 0, 2 * DN_HEADS, pad_s, halo_s, tiles_per_seq, _l2norm, qk_out)
    _project_between(h_ref, w_ref, o_ref, stages, 2 * DN_HEADS // PROJ_PARTS)


def _proj4_kernel(h_ref, w_ref, v_ref, cw_ref, o_ref, v_out, pad_s, halo_s, *, tiles_per_seq):
    stages = _conv_silu_tile(v_ref, cw_ref, 2 * DN_WIDTH, DN_HEADS, pad_s, halo_s, tiles_per_seq, lambda t: t, v_out)
    _project_between(h_ref, w_ref, o_ref, stages, DN_HEADS // PROJ_PARTS)


def _in_proj(x2, gain, w_main, w_tail, conv_w, pos, inv128, qnn, qrn, cn, knn, krn, wkv, b, s):
    t = x2.shape[0]
    tm = min(1024, s)
    tn = IN_PROJ_TN
    nsb = s // tm
    nh = MLA_HEADS
    rows = lambda w, blk=0: pl.BlockSpec((tm, w), lambda i: (i, blk))
    w_tile = lambda j: _resident((D_MODEL, tn))
    heads = lambda w: pl.BlockSpec((1, nh, tm, w), lambda i: (i // nsb, 0, i % nsb, 0))
    tile = jax.ShapeDtypeStruct((t, tn), BF16)
    head_shape = lambda w: jax.ShapeDtypeStruct((b, nh, s, w), BF16)
    conv_scratch = lambda width: [pltpu.VMEM((2, CONV_PAD + tm, DN_HEAD_DIM), F32), pltpu.VMEM((CONV_PAD, width), F32)]
    grid = (t // tm,)

    h, p0, tail, trig = pl.pallas_call(
        _proj0_kernel, grid=grid,
        in_specs=[rows(D_MODEL), _resident((1, D_MODEL)), w_tile(0), _resident((D_MODEL, LANES)), rows(1),
                  _resident((1, 2 * MLA_ROPE))],
        out_specs=[rows(D_MODEL), rows(tn), rows(LANES), rows(2 * MLA_ROPE)],
        out_shape=[jax.ShapeDtypeStruct((t, D_MODEL), BF16), tile, jax.ShapeDtypeStruct((t, LANES), F32),
                   jax.ShapeDtypeStruct((t, 2 * MLA_ROPE), F32)],
        compiler_params=_cparams("parallel"), name="in_proj0",
    )(x2, gain, w_main[0], w_tail, pos, inv128)

    p1, q_b = pl.pallas_call(
        _proj1_kernel, grid=grid,
        in_specs=[rows(D_MODEL), w_tile(1), rows(MQ_W), rows(2 * MLA_ROPE), _resident((1, MLA_NOPE)),
                  _resident((1, MLA_ROPE))],
        out_specs=[rows(tn), heads(MLA_QK)],
        out_shape=[tile, head_shape(MLA_QK)],
        compiler_params=_cparams("parallel"), name="in_proj1",
    )(h, w_main[1], p0, trig, qnn, qrn)

    p2, k_b, v_b = pl.pallas_call(
        _proj2_kernel, grid=grid,
        in_specs=[rows(D_MODEL), w_tile(2), rows(KV_RANK, CKV_BLK), rows(LANES), rows(2 * MLA_ROPE),
                  _resident((1, KV_RANK)), _resident((1, MLA_NOPE)), _resident((1, MLA_ROPE)),
                  _resident((KV_RANK, nh * (MLA_NOPE + MLA_V)))],
        out_specs=[rows(tn), heads(MLA_QK), heads(MLA_V)],
        out_shape=[tile, head_shape(MLA_QK), head_shape(MLA_V)],
        compiler_params=_cparams("parallel"), name="in_proj2",
    )(h, w_main[2], p0, tail, trig, cn, knn, krn, wkv)

    gate_a, qk_n = pl.pallas_call(
        functools.partial(_proj3_kernel, tiles_per_seq=nsb), grid=grid,
        in_specs=[rows(D_MODEL), w_tile(3), rows(tn), _resident((CONV_K, 3 * DN_WIDTH))],
        out_specs=[rows(tn), rows(2 * DN_WIDTH)],
        out_shape=[tile, jax.ShapeDtypeStruct((t, 2 * DN_WIDTH), BF16)],
        scratch_shapes=conv_scratch(2 * DN_WIDTH),
        compiler_params=_cparams("arbitrary"), name="in_proj3",
    )(h, w_main[3], p1, conv_w)

    gate_b, v_n = pl.pallas_call(
        functools.partial(_proj4_kernel, tiles_per_seq=nsb), grid=grid,
        in_specs=[rows(D_MODEL), w_tile(4), rows(DN_WIDTH), _resident((CONV_K, 3 * DN_WIDTH))],
        out_specs=[rows(tn), rows(DN_WIDTH)],
        out_shape=[tile, jax.ShapeDtypeStruct((t, DN_WIDTH), BF16)],
        scratch_shapes=conv_scratch(DN_WIDTH),
        compiler_params=_cparams("arbitrary"), name="in_proj4",
    )(h, w_main[4], p2, conv_w)

    return dict(tail=tail, z_tile=p2, gate_a=gate_a, gate_b=gate_b, q_b=q_b, k_b=k_b, v_b=v_b, qk_n=qk_n, v_n=v_n)


GDN_GROUP = 4
GDN_PREP_CHUNKS = 4


def _dot_r(x, m):
    return _dot(x.astype(BF16), m.astype(BF16))


def _gdn_kernel(alog_ref, dtb_ref, q_ref, k_ref, v_ref, z_ref, t_ref, on_ref,
                o_ref,
                g_s, k_s, kq_s, kv_s, wq_s, kd_s, u_s, qk_s, st_s):
    s = q_ref.shape[0]
    n_chunks = s // CHUNK
    hd = DN_HEAD_DIM
    group = pl.program_id(1)
    pos = lax.broadcasted_iota(jnp.int32, (s, 1), 0) % CHUNK
    lane = lax.broadcasted_iota(jnp.int32, (1, LANES), 1)

    def chunked(t):
        return t.reshape(n_chunks, CHUNK, t.shape[-1])

    def column(t, c):
        return jnp.sum(jnp.where(lane == c, t, 0.0), axis=1, keepdims=True)

    tail = t_ref[...]
    beta_all = jax.nn.sigmoid(tail)
    xs = tail + dtb_ref[...]
    softplus = jnp.maximum(xs, 0.0) + jnp.log1p(jnp.exp(-jnp.abs(xs)))
    gc_all = -jnp.exp(alog_ref[...]) * softplus
    step = 1
    while step < CHUNK:
        gc_all = gc_all + jnp.where(pos >= step, pltpu.roll(gc_all, step, 0), 0.0)
        step *= 2

    for hh in range(GDN_GROUP):
        head = group * GDN_GROUP + hh
        c0 = hh * hd
        qn = q_ref[:, c0:c0 + hd].astype(F32)
        kn = k_ref[:, c0:c0 + hd].astype(F32)
        vv = v_ref[:, c0:c0 + hd].astype(F32)

        beta = column(beta_all, TAIL_BETA + head)
        gc = column(gc_all, TAIL_ALPHA + head)
        gc3 = chunked(gc)
        g_last = jnp.broadcast_to(gc3[:, CHUNK - 1:CHUNK, :], gc3.shape).reshape(s, 1)
        eg = jnp.exp(gc)

        qs = qn * (hd ** -0.5)
        kb = kn * beta
        g_s[hh] = jnp.broadcast_to(gc, (s, LANES))
        k_s[hh] = kn.astype(BF16)
        kq_s[hh, :, :CHUNK, :] = chunked(kb.astype(BF16))
        kq_s[hh, :, CHUNK:, :] = chunked(qs.astype(BF16))
        kv_s[hh, :, :hd] = (kb * eg).astype(BF16)
        kv_s[hh, :, hd:] = (vv * beta).astype(BF16)
        wq_s[hh, :, CHUNK:, :] = chunked((qs * eg).astype(BF16))
        kd_s[hh] = (kn * jnp.exp(g_last - gc)).astype(BF16)
        st_s[hh] = jnp.zeros((hd, hd), F32)

    ri = lax.broadcasted_iota(jnp.int32, (CHUNK, CHUNK), 0)
    ci = lax.broadcasted_iota(jnp.int32, (CHUNK, CHUNK), 1)
    eye = ri == ci
    eye_f = jnp.where(eye, 1.0, 0.0).astype(F32)
    n_factors = CHUNK.bit_length() - 2

    heads = range(GDN_GROUP)

    def prep(it):
        items = [(hh, it * GDN_PREP_CHUNKS + c) for c in range(GDN_PREP_CHUNKS) for hh in heads]
        sls = [pl.ds(pl.multiple_of(n * CHUNK, CHUNK), CHUNK) for _, n in items]
        rs = [_dot_nt(kq_s[hh, n], k_s[hh, sl, :]) for (hh, n), sl in zip(items, sls)]
        yield
        ms, ps = [], []
        for (hh, n), sl, r in zip(items, sls, rs):
            g_col = g_s[hh, sl, :][:, :CHUNK]
            g_row = jnp.sum(jnp.where(eye, g_col, 0.0), axis=0, keepdims=True)
            decay = jnp.exp(jnp.where(ri >= ci, g_col - g_row, -jnp.inf))
            a = jnp.where(ri > ci, r[:CHUNK] * decay, 0.0)
            qk_s[hh, sl, :] = (r[CHUNK:] * decay).astype(BF16)
            ms.append(a)
            ps.append(eye_f - a)
        ms = [_dot_r(a, a) for a in ms]
        yield
        for _ in range(n_factors - 1):
            rs = [_dot_r(jnp.concatenate([m, p], axis=0), m) for m, p in zip(ms, ps)]
            ms = [r[:CHUNK] for r in rs]
            ps = [p + r[CHUNK:] for p, r in zip(ps, rs)]
            yield
        rs = [_dot_r(p, m) for m, p in zip(ms, ps)]
        yield
        wus = [_dot((p + r).astype(BF16), kv_s[hh, sl, :]) for (hh, _), sl, p, r in zip(items, sls, ps, rs)]
        for (hh, n), sl, wu in zip(items, sls, wus):
            wq_s[hh, n, :CHUNK, :] = wu[:, :hd].astype(BF16)
            u_s[hh, sl, :] = wu[:, hd:]
        yield

    def scan(n):
        r0 = pl.multiple_of(n * CHUNK, CHUNK)
        sl = pl.ds(r0, CHUNK)
        states = [st_s[hh] for hh in heads]
        rs = [_dot(wq_s[hh, n], states[hh].astype(BF16)) for hh in heads]
        yield
        v_nbs = [(u_s[hh, sl, :] - rs[hh][:CHUNK]).astype(BF16) for hh in heads]
        os = [rs[hh][CHUNK:] + _dot(qk_s[hh, sl, :], v_nbs[hh]) for hh in heads]
        kvs = [_dot_tn(kd_s[hh, sl, :], v_nbs[hh]) for hh in heads]
        yield
        for hh in heads:
            decay_last = jnp.exp(g_s[hh, pl.ds(r0 + CHUNK - 1, 1), :])
            st_s[hh] = states[hh] * decay_last + kvs[hh]
            z = z_ref[sl, hh * hd:(hh + 1) * hd].astype(F32)
            o_ref[sl, hh * hd:(hh + 1) * hd] = (_rms(os[hh], on_ref[...]) * (z * jax.nn.sigmoid(z))).astype(o_ref.dtype)

    n_groups = n_chunks // GDN_PREP_CHUNKS
    for _ in prep(0):
        pass

    def body(it, carry):
        prep_stages = prep((it + 1) % n_groups)
        for c in range(GDN_PREP_CHUNKS):
            for _ in scan(it * GDN_PREP_CHUNKS + c):
                next(prep_stages, None)
        for _ in prep_stages:
            pass
        return carry

    lax.fori_loop(0, n_groups, body, 0)


def _gdn(qk_n, v_n, z_tile, tail, a_log, dt_bias, out_norm, b, s):
    hd = DN_HEAD_DIM
    gw = hd * GDN_GROUP
    n_groups = DN_HEADS // GDN_GROUP
    n_chunks = s // CHUNK
    col = lambda k: pl.BlockSpec((s, gw), lambda bi, g: (bi, k * n_groups + g))
    per_head = lambda shape, dt: pltpu.VMEM((GDN_GROUP,) + shape, dt)
    lane_row = lambda a: jnp.zeros((1, LANES), F32).at[0, TAIL_ALPHA:TAIL_ALPHA + DN_HEADS].set(a.astype(F32))
    return pl.pallas_call(
        _gdn_kernel,
        grid=(b, n_groups),
        in_specs=[
            _resident((1, LANES)), _resident((1, LANES)),
            col(0), col(1), col(0), col(1),
            pl.BlockSpec((s, LANES), lambda bi, g: (bi, 0)),
            _resident((1, hd)),
        ],
        out_specs=pl.BlockSpec((s, gw), lambda bi, g: (bi, g)),
        out_shape=jax.ShapeDtypeStruct((b * s, DN_WIDTH), BF16),
        scratch_shapes=[
            per_head((s, LANES), F32),
            per_head((s, hd), BF16),
            per_head((n_chunks, 2 * CHUNK, hd), BF16),
            per_head((s, 2 * hd), BF16),
            per_head((n_chunks, 2 * CHUNK, hd), BF16),
            per_head((s, hd), BF16),
            per_head((s, hd), F32),
            per_head((s, CHUNK), BF16),
            per_head((hd, hd), F32),
        ],
        compiler_params=_cparams("parallel", "arbitrary"),
        name="gdn",
    )(lane_row(a_log), lane_row(dt_bias), qk_n, qk_n, v_n, z_tile, tail, out_norm)


ATTN_LOOKAHEAD = 1


def _attn_kernel(q_ref, k_ref, v_ref, o_ref, v1_s, *, tq):
    s = q_ref.shape[2]
    v1_s[:, :MLA_V] = v_ref[0, 0]
    v1_s[:, MLA_V:] = jnp.ones((s, MLA_V), BF16)
    ri = lax.broadcasted_iota(jnp.int32, (tq, tq), 0) // CHUNK
    ci = lax.broadcasted_iota(jnp.int32, (tq, tq), 1) // CHUNK
    allowed = ci <= ri
    n_q = s // tq

    def scores(qi):
        lo, hi = qi * tq, (qi + 1) * tq
        q = q_ref[0, 0, lo:hi, :]
        s_d = jnp.where(allowed, _dot_nt(q, k_ref[0, 0, lo:hi, :]), -jnp.inf)
        s_o = _dot_nt(q, k_ref[0, 0, :lo, :]) if qi > 0 else None
        return s_d, s_o

    pending = [scores(qi) for qi in range(min(ATTN_LOOKAHEAD, n_q))]
    for qi in range(n_q):
        lo, hi = qi * tq, (qi + 1) * tq
        if qi + ATTN_LOOKAHEAD < n_q:
            pending.append(scores(qi + ATTN_LOOKAHEAD))
        s_d, s_o = pending.pop(0)
        m = jnp.max(s_d, axis=-1, keepdims=True)
        if s_o is not None:
            m = jnp.maximum(m, jnp.max(s_o, axis=-1, keepdims=True))
        acc = _dot(jnp.exp2(s_d - m).astype(BF16), v1_s[lo:hi, :])
        if s_o is not None:
            acc = acc + _dot(jnp.exp2(s_o - m).astype(BF16), v1_s[:lo, :])
        o_ref[lo:hi, :] = (acc[:, :MLA_V] / acc[:, MLA_V:]).astype(o_ref.dtype)


def _mla_attn(q, k, v):
    b, nh, s, _ = q.shape
    tq = min(256, s)
    spec = lambda w: pl.BlockSpec((1, 1, s, w), lambda bi, h: (bi, h, 0, 0))
    return pl.pallas_call(
        functools.partial(_attn_kernel, tq=tq),
        grid=(b, nh),
        in_specs=[spec(MLA_QK), spec(MLA_QK), spec(MLA_V)],
        out_specs=pl.BlockSpec((s, MLA_V), lambda bi, h: (bi, h)),
        out_shape=jax.ShapeDtypeStruct((b * s, MLA_WIDTH), BF16),
        scratch_shapes=[pltpu.VMEM((s, 2 * MLA_V), BF16)],
        compiler_params=_cparams("parallel", "parallel"),
        name="mla_attn",
    )(q, k, v)


def _merge_kernel(x_ref, oa_ref, ob_ref, ga_ref, gb_ref, wa_ref, wb_ref, wo_ref, o_ref):
    ya = jax.nn.sigmoid(ga_ref[...].astype(F32)) * _dot(oa_ref[...], wa_ref[...])
    yb = jax.nn.sigmoid(gb_ref[...].astype(F32)) * _dot(ob_ref[...], wb_ref[...])
    o_ref[...] = x_ref[...] + _dot((ya + yb).astype(BF16), wo_ref[...])


def _merge(x2, o_a, o_b, gate_a, gate_b, w_a, w_b, w_o):
    t = x2.shape[0]
    tm = min(512, t)
    row = lambda w, blk: pl.BlockSpec((tm, w), lambda i: (i, blk))
    return pl.pallas_call(
        _merge_kernel,
        grid=(t // tm,),
        in_specs=[
            row(D_MODEL, 0), row(DN_WIDTH, 0), row(MLA_WIDTH, 0), row(D_MODEL, 0), row(D_MODEL, 0),
            _resident((DN_WIDTH, D_MODEL)), _resident((MLA_WIDTH, D_MODEL)), _resident((D_MODEL, D_MODEL)),
        ],
        out_specs=row(D_MODEL, 0),
        out_shape=jax.ShapeDtypeStruct((t, D_MODEL), F32),
        compiler_params=_cparams("parallel"),
        name="merge",
    )(x2, o_a, o_b, gate_a, gate_b, w_a, w_b, w_o)


def _mlp_kernel(x_ref, gm_ref, up_ref, dn_ref, o_ref, hm_ref):
    @pl.when(pl.program_id(1) == 0)
    def _():
        x = x_ref[...]
        hm_ref[...] = _rms(x, gm_ref[...]).astype(BF16)
        o_ref[...] = x

    a = _dot(hm_ref[...], up_ref[...])
    o_ref[...] += _dot(jnp.square(jnp.maximum(a, 0.0)).astype(BF16), dn_ref[...])


def _mlp(x1, gm, w_up, w_dn):
    t = x1.shape[0]
    tm = min(1024, t)
    tf = 1024
    return pl.pallas_call(
        _mlp_kernel,
        grid=(t // tm, D_FF // tf),
        in_specs=[
            pl.BlockSpec((tm, D_MODEL), lambda i, f: (i, 0)),
            _resident((1, D_MODEL)),
            pl.BlockSpec((D_MODEL, tf), lambda i, f: (0, f)),
            pl.BlockSpec((tf, D_MODEL), lambda i, f: (f, 0)),
        ],
        out_specs=pl.BlockSpec((tm, D_MODEL), lambda i, f: (i, 0)),
        out_shape=jax.ShapeDtypeStruct((t, D_MODEL), F32),
        scratch_shapes=[pltpu.VMEM((tm, D_MODEL), BF16)],
        compiler_params=_cparams("parallel", "arbitrary"),
        name="mlp",
    )(x1, gm, w_up, w_dn)


def _ple_kernel(x_ref, gp_ref, wg_ref, p_ref, wp_ref, o_ref):
    x = x_ref[...]
    gate = jax.nn.sigmoid(_dot(_rms(x, gp_ref[...]).astype(BF16), wg_ref[...]))
    o_ref[...] = x + gate * _dot(p_ref[...].astype(BF16), wp_ref[...])


def _ple(x2, gp, w_gate, p2, w_ple):
    t = x2.shape[0]
    tm = min(1024, t)
    row = lambda w: pl.BlockSpec((tm, w), lambda i: (i, 0))
    return pl.pallas_call(
        _ple_kernel,
        grid=(t // tm,),
        in_specs=[row(D_MODEL), _resident((1, D_MODEL)), _resident((D_MODEL, D_MODEL)), row(PLE_DIM),
                  _resident((PLE_DIM, D_MODEL))],
        out_specs=row(D_MODEL),
        out_shape=jax.ShapeDtypeStruct((t, D_MODEL), F32),
        compiler_params=_cparams("parallel"),
        name="ple",
    )(x2, gp, w_gate, p2, w_ple)


def _regroup_w_in(w):
    d = w.shape[0]
    mq = w[:, _OFF_MQ:_OFF_CKV].reshape(d, MLA_HEADS, MLA_QK)
    tiles = (
        jnp.concatenate([
            mq[:, :, :MLA_NOPE].reshape(d, MLA_HEADS * MLA_NOPE),
            mq[:, :, MLA_NOPE:].reshape(d, MLA_HEADS * MLA_ROPE),
            w[:, _OFF_CKV:_OFF_KR],
        ], axis=1),
        w[:, _OFF_Q:_OFF_V],
        w[:, _OFF_V:_OFF_BETA],
        w[:, _OFF_GA:_OFF_GB],
        w[:, _OFF_GB:D_IN],
    )
    tail = jnp.concatenate([
        w[:, _OFF_KR:_OFF_GA],
        w[:, _OFF_BETA:_OFF_MQ],
        jnp.zeros((d, LANES - MLA_ROPE - 2 * DN_HEADS), w.dtype),
    ], axis=1)
    return tuple(tile.astype(BF16) for tile in tiles), tail.astype(BF16)


def kernel(x, p, positions, mix_norm, w_in, conv_w, dt_bias, a_log, dn_out_norm, ckv_norm, w_kv_up, q_nope_norm,
           q_rope_norm, k_nope_norm, k_rope_norm, w_branch_a, w_branch_b, w_out, mlp_norm, w_mlp_up, w_mlp_down,
           ple_norm, w_ple_gate, w_ple):
    b, s, d = x.shape
    t = b * s
    depth = w_in.shape[0]
    half = MLA_ROPE // 2
    inv = ROPE_BASE ** (-jnp.arange(half, dtype=F32) / half)
    inv64 = jnp.concatenate([inv, inv, inv, inv]).reshape(1, 2 * MLA_ROPE)
    pos = positions.reshape(t, 1)
    row = lambda a: a.reshape(1, -1).astype(F32)

    x2 = x.reshape(t, d)
    for i in range(depth):
        w_main, w_tail = _regroup_w_in(w_in[i])
        pr = _in_proj(
            x2, row(mix_norm[i]), w_main, w_tail, conv_w[i], pos, inv64, row(q_nope_norm[i]), row(q_rope_norm[i]),
            row(ckv_norm[i]), row(k_nope_norm[i]), row(k_rope_norm[i]), w_kv_up[i].astype(BF16), b, s)
        o_a = _gdn(pr["qk_n"], pr["v_n"], pr["z_tile"], pr["tail"], a_log[i], dt_bias[i], row(dn_out_norm[i]), b, s)
        o_b = _mla_attn(pr["q_b"], pr["k_b"], pr["v_b"])
        x1 = _merge(x2, o_a, o_b, pr["gate_a"], pr["gate_b"], w_branch_a[i].astype(BF16),
                    w_branch_b[i].astype(BF16), w_out[i].astype(BF16))
        x2 = _mlp(x1, row(mlp_norm[i]), w_mlp_up[i].astype(BF16), w_mlp_down[i].astype(BF16))
        x2 = _ple(x2, row(ple_norm[i]), w_ple_gate[i].astype(BF16), p[i].reshape(t, PLE_DIM), w_ple[i].astype(BF16))
    return x2.reshape(b, s, d)
```

```python
import functools

import jax
import jax.numpy as jnp
from jax import lax
from jax.experimental import pallas as pl
from jax.experimental.pallas import tpu as pltpu

F32 = jnp.float32
BF16 = jnp.bfloat16

D_MODEL = 2048
CHUNK = 64
PLE_DIM = 256
EPS = 1e-6
DN_HEADS = 8
DN_HEAD_DIM = 128
DN_WIDTH = DN_HEADS * DN_HEAD_DIM
CONV_K = 4
MLA_HEADS = 8
MLA_NOPE = 128
MLA_ROPE = 64
MLA_QK = MLA_NOPE + MLA_ROPE
MLA_V = 128
KV_RANK = 512
MLA_WIDTH = MLA_HEADS * MLA_V
ROPE_BASE = 10000.0
D_FF = 4 * D_MODEL
LOG2_E = 1.4426950408889634
HALF_PI = 1.5707963267948966

_OFF_Q = 0
_OFF_K = _OFF_Q + DN_WIDTH
_OFF_V = _OFF_K + DN_WIDTH
_OFF_Z = _OFF_V + DN_WIDTH
_OFF_BETA = _OFF_Z + DN_WIDTH
_OFF_ALPHA = _OFF_BETA + DN_HEADS
_OFF_MQ = _OFF_ALPHA + DN_HEADS
_OFF_CKV = _OFF_MQ + MLA_HEADS * MLA_QK
_OFF_KR = _OFF_CKV + KV_RANK
_OFF_GA = _OFF_KR + MLA_ROPE
_OFF_GB = _OFF_GA + D_MODEL
D_IN = _OFF_GB + D_MODEL

MQ_W = MLA_HEADS * MLA_QK
CKV_BLK = MQ_W // KV_RANK
LANES = 128
TAIL_BETA = MLA_ROPE
TAIL_ALPHA = MLA_ROPE + DN_HEADS

VMEM_LIMIT = 60 * 1024 * 1024


def _cparams(*sem):
    return pltpu.CompilerParams(dimension_semantics=sem, vmem_limit_bytes=VMEM_LIMIT)


def _rms(t, g):
    return t * lax.rsqrt(jnp.mean(t * t, axis=-1, keepdims=True) + EPS) * g


def _dot(a, b):
    return jnp.dot(a, b, preferred_element_type=F32)


def _dot_nt(a, b):
    return lax.dot_general(a, b, (((1,), (1,)), ((), ())), preferred_element_type=F32)


def _dot_tn(a, b):
    return lax.dot_general(a, b, (((0,), (0,)), ((), ())), preferred_element_type=F32)


def _resident(shape):
    nd = len(shape)
    return pl.BlockSpec(shape, lambda *_: (0,) * nd, pipeline_mode=pl.Buffered(1))


IN_PROJ_TN = 2048
assert MQ_W + KV_RANK == IN_PROJ_TN and 2 * DN_WIDTH == IN_PROJ_TN and D_MODEL == IN_PROJ_TN
CONV_PAD = 8
CONV_ROWS = 128
PROJ0_ROWS = 256


def _l2norm(t):
    return t * lax.rsqrt(jnp.sum(t * t, axis=-1, keepdims=True) + EPS)


def _rope(t, trig):
    half = MLA_ROPE // 2
    rot = jnp.concatenate([-t[:, half:], t[:, :half]], axis=1)
    return t * trig[:, :MLA_ROPE] + rot * trig[:, MLA_ROPE:]


def _proj0_kernel(x_ref, g_ref, w_ref, wt_ref, pos_ref, inv_ref, h_out, o_ref, t_ref, trig_out):
    block = min(PROJ0_ROWS, x_ref.shape[0])
    for r0 in range(0, x_ref.shape[0], block):
        rows = slice(r0, r0 + block)
        h = _rms(x_ref[rows, :], g_ref[...]).astype(BF16)
        h_out[rows, :] = h
        t_ref[rows, :] = _dot(h, wt_ref[...])
        o_ref[rows, :] = _dot(h, w_ref[...]).astype(BF16)
    ang = pos_ref[...].astype(F32) * inv_ref[...]
    trig_out[...] = jnp.cos(jnp.where(lax.broadcasted_iota(jnp.int32, ang.shape, 1) < MLA_ROPE, ang, ang - HALF_PI))


def _proj1_kernel(h_ref, w_ref, mq_ref, trig_ref, qnn_ref, qrn_ref, t_ref, alog_ref, dtb_ref,
                  o_ref, q_out, beta_out, gc_out):
    o_ref[...] = _dot(h_ref[...], w_ref[...]).astype(BF16)
    tail = t_ref[...]
    beta_out[...] = jax.nn.sigmoid(tail)
    xs = tail + dtb_ref[...]
    softplus = jnp.maximum(xs, 0.0) + jnp.log1p(jnp.exp(-jnp.abs(xs)))
    gc = -jnp.exp(alog_ref[...]) * softplus
    pos = lax.broadcasted_iota(jnp.int32, (tail.shape[0], 1), 0) % CHUNK
    step = 1
    while step < CHUNK:
        gc = gc + jnp.where(pos >= step, pltpu.roll(gc, step, 0), 0.0)
        step *= 2
    gc_out[...] = gc
    scale = MLA_QK ** -0.5 * LOG2_E
    trig = trig_ref[...]
    rope0 = MLA_HEADS * MLA_NOPE
    for h in range(MLA_HEADS):
        qn = mq_ref[:, h * MLA_NOPE:(h + 1) * MLA_NOPE].astype(F32)
        qr = mq_ref[:, rope0 + h * MLA_ROPE:rope0 + (h + 1) * MLA_ROPE].astype(F32)
        q_out[0, h, :, :MLA_NOPE] = (_rms(qn, qnn_ref[...]) * scale).astype(BF16)
        q_out[0, h, :, MLA_NOPE:] = (_rope(_rms(qr, qrn_ref[...]), trig) * scale).astype(BF16)


def _proj2_kernel(h_ref, w_ref, ckv_ref, t_ref, trig_ref, cn_ref, knn_ref, krn_ref, wkv_ref, o_ref, k_out, v_out):
    o_ref[...] = _dot(h_ref[...], w_ref[...]).astype(BF16)
    kr = _rope(_rms(t_ref[...][:, :MLA_ROPE], krn_ref[...]), trig_ref[...]).astype(BF16)
    c = _rms(ckv_ref[...].astype(F32), cn_ref[...]).astype(BF16)
    kv = _dot(c, wkv_ref[...])
    for h in range(MLA_HEADS):
        kv0 = h * (MLA_NOPE + MLA_V)
        k_out[0, h, :, :MLA_NOPE] = _rms(kv[:, kv0:kv0 + MLA_NOPE], knn_ref[...]).astype(BF16)
        k_out[0, h, :, MLA_NOPE:] = kr
        v_out[0, h, :, :] = kv[:, kv0 + MLA_NOPE:kv0 + MLA_NOPE + MLA_V].astype(BF16)


def _conv_silu_tile(src_ref, cw_ref, cw_col0, n_slices, pad_s, halo_s, tiles_per_seq, finish, dst_ref):
    tm = src_ref.shape[0]
    i = pl.program_id(0)

    @pl.when(i == 0)
    def _():
        halo_s[...] = jnp.zeros(halo_s.shape, F32)

    keep = jnp.where(i % tiles_per_seq == 0, 0.0, 1.0).astype(F32)
    for n in range(n_slices):
        yield
        cols = slice(n * DN_HEAD_DIM, (n + 1) * DN_HEAD_DIM)
        slot = n % pad_s.shape[0]
        w = cw_ref[:, cw_col0 + n * DN_HEAD_DIM:cw_col0 + (n + 1) * DN_HEAD_DIM]
        pad_s[slot, :CONV_PAD, :] = halo_s[:, cols] * keep
        pad_s[slot, CONV_PAD:, :] = src_ref[:, cols].astype(F32)
        halo_s[:, cols] = pad_s[slot, tm:tm + CONV_PAD, :]
        for r0 in range(0, tm, CONV_ROWS):
            acc = pad_s[slot, CONV_PAD + r0:CONV_PAD + r0 + CONV_ROWS, :] * w[CONV_K - 1:CONV_K, :]
            for d in range(1, CONV_K):
                lo = CONV_PAD + r0 - d
                acc = acc + pad_s[slot, lo:lo + CONV_ROWS, :] * w[CONV_K - 1 - d:CONV_K - d, :]
            dst_ref[r0:r0 + CONV_ROWS, cols] = finish(acc * jax.nn.sigmoid(acc)).astype(BF16)


PROJ_PARTS = 8


def _project_between(h_ref, w_ref, o_ref, stages, stages_per_part):
    width = o_ref.shape[1] // PROJ_PARTS
    next(stages)
    for part in range(PROJ_PARTS):
        cols = slice(part * width, (part + 1) * width)
        o_ref[:, cols] = _dot(h_ref[...], w_ref[:, cols]).astype(BF16)
        for _ in range(stages_per_part):
            next(stages, None)


def _proj3_kernel(h_ref, w_ref, qk_ref, cw_ref, o_ref, qk_out, pad_s, halo_s, *, tiles_per_seq):
    stages = _conv_silu_tile(qk_ref, cw_ref, 0, 2 * DN_HEADS, pad_s, halo_s, tiles_per_seq, _l2norm, qk_out)
    _project_between(h_ref, w_ref, o_ref, stages, 2 * DN_HEADS // PROJ_PARTS)


def _proj4_kernel(h_ref, w_ref, v_ref, cw_ref, o_ref, v_out, pad_s, halo_s, *, tiles_per_seq):
    stages = _conv_silu_tile(v_ref, cw_ref, 2 * DN_WIDTH, DN_HEADS, pad_s, halo_s, tiles_per_seq, lambda t: t, v_out)
    _project_between(h_ref, w_ref, o_ref, stages, DN_HEADS // PROJ_PARTS)


def _in_proj(x2, gain, w_main, w_tail, conv_w, a_log, dt_bias, pos, inv128, qnn, qrn, cn, knn, krn, wkv, b, s):
    t = x2.shape[0]
    tm = min(1024, s)
    tn = IN_PROJ_TN
    nsb = s // tm
    nh = MLA_HEADS
    rows = lambda w, blk=0: pl.BlockSpec((tm, w), lambda i: (i, blk))
    w_tile = lambda j: _resident((D_MODEL, tn))
    heads = lambda w: pl.BlockSpec((1, nh, tm, w), lambda i: (i // nsb, 0, i % nsb, 0))
    tile = jax.ShapeDtypeStruct((t, tn), BF16)
    head_shape = lambda w: jax.ShapeDtypeStruct((b, nh, s, w), BF16)
    conv_scratch = lambda width: [pltpu.VMEM((2, CONV_PAD + tm, DN_HEAD_DIM), F32), pltpu.VMEM((CONV_PAD, width), F32)]
    grid = (t // tm,)

    h, p0, tail, trig = pl.pallas_call(
        _proj0_kernel, grid=grid,
        in_specs=[rows(D_MODEL), _resident((1, D_MODEL)), w_tile(0), _resident((D_MODEL, LANES)), rows(1),
                  _resident((1, 2 * MLA_ROPE))],
        out_specs=[rows(D_MODEL), rows(tn), rows(LANES), rows(2 * MLA_ROPE)],
        out_shape=[jax.ShapeDtypeStruct((t, D_MODEL), BF16), tile, jax.ShapeDtypeStruct((t, LANES), F32),
                   jax.ShapeDtypeStruct((t, 2 * MLA_ROPE), F32)],
        compiler_params=_cparams("parallel"), name="in_proj0",
    )(x2, gain, w_main[0], w_tail, pos, inv128)

    lane_row = lambda a: jnp.zeros((1, LANES), F32).at[0, TAIL_ALPHA:TAIL_ALPHA + DN_HEADS].set(a.astype(F32))
    lanes_f32 = jax.ShapeDtypeStruct((t, LANES), F32)
    p1, q_b, beta_all, gc_all = pl.pallas_call(
        _proj1_kernel, grid=grid,
        in_specs=[rows(D_MODEL), w_tile(1), rows(MQ_W), rows(2 * MLA_ROPE), _resident((1, MLA_NOPE)),
                  _resident((1, MLA_ROPE)), rows(LANES), _resident((1, LANES)), _resident((1, LANES))],
        out_specs=[rows(tn), heads(MLA_QK), rows(LANES), rows(LANES)],
        out_shape=[tile, head_shape(MLA_QK), lanes_f32, lanes_f32],
        compiler_params=_cparams("parallel"), name="in_proj1",
    )(h, w_main[1], p0, trig, qnn, qrn, tail, lane_row(a_log), lane_row(dt_bias))

    p2, k_b, v_b = pl.pallas_call(
        _proj2_kernel, grid=grid,
        in_specs=[rows(D_MODEL), w_tile(2), rows(KV_RANK, CKV_BLK), rows(LANES), rows(2 * MLA_ROPE),
                  _resident((1, KV_RANK)), _resident((1, MLA_NOPE)), _resident((1, MLA_ROPE)),
                  _resident((KV_RANK, nh * (MLA_NOPE + MLA_V)))],
        out_specs=[rows(tn), heads(MLA_QK), heads(MLA_V)],
        out_shape=[tile, head_shape(MLA_QK), head_shape(MLA_V)],
        compiler_params=_cparams("parallel"), name="in_proj2",
    )(h, w_main[2], p0, tail, trig, cn, knn, krn, wkv)

    gate_a, qk_n = pl.pallas_call(
        functools.partial(_proj3_kernel, tiles_per_seq=nsb), grid=grid,
        in_specs=[rows(D_MODEL), w_tile(3), rows(tn), _resident((CONV_K, 3 * DN_WIDTH))],
        out_specs=[rows(tn), rows(2 * DN_WIDTH)],
        out_shape=[tile, jax.ShapeDtypeStruct((t, 2 * DN_WIDTH), BF16)],
        scratch_shapes=conv_scratch(2 * DN_WIDTH),
        compiler_params=_cparams("arbitrary"), name="in_proj3",
    )(h, w_main[3], p1, conv_w)

    gate_b, v_n = pl.pallas_call(
        functools.partial(_proj4_kernel, tiles_per_seq=nsb), grid=grid,
        in_specs=[rows(D_MODEL), w_tile(4), rows(DN_WIDTH), _resident((CONV_K, 3 * DN_WIDTH))],
        out_specs=[rows(tn), rows(DN_WIDTH)],
        out_shape=[tile, jax.ShapeDtypeStruct((t, DN_WIDTH), BF16)],
        scratch_shapes=conv_scratch(DN_WIDTH),
        compiler_params=_cparams("arbitrary"), name="in_proj4",
    )(h, w_main[4], p2, conv_w)

    return dict(beta_all=beta_all, gc_all=gc_all, z_tile=p2, gate_a=gate_a, gate_b=gate_b, q_b=q_b, k_b=k_b, v_b=v_b,
                qk_n=qk_n, v_n=v_n)


GDN_GROUP = 4
GDN_PREP_CHUNKS = 4


def _dot_r(x, m):
    return _dot(x.astype(BF16), m.astype(BF16))


def _gdn_kernel(q_ref, k_ref, v_ref, z_ref, beta_ref, gc_ref, on_ref,
                o_ref,
                g_s, k_s, kq_s, kv_s, wq_s, kd_s, u_s, qk_s, st_s):
    s = q_ref.shape[0]
    n_chunks = s // CHUNK
    hd = DN_HEAD_DIM
    group = pl.program_id(1)
    lane = lax.broadcasted_iota(jnp.int32, (1, LANES), 1)

    def chunked(t):
        return t.reshape(n_chunks, CHUNK, t.shape[-1])

    def column(t, c):
        return jnp.sum(jnp.where(lane == c, t, 0.0), axis=1, keepdims=True)

    beta_all = beta_ref[...]
    gc_all = gc_ref[...]

    for hh in range(GDN_GROUP):
        head = group * GDN_GROUP + hh
        c0 = hh * hd
        qn = q_ref[:, c0:c0 + hd].astype(F32)
        kn = k_ref[:, c0:c0 + hd].astype(F32)
        vv = v_ref[:, c0:c0 + hd].astype(F32)

        beta = column(beta_all, TAIL_BETA + head)
        gc = column(gc_all, TAIL_ALPHA + head)
        gc3 = chunked(gc)
        g_last = jnp.broadcast_to(gc3[:, CHUNK - 1:CHUNK, :], gc3.shape).reshape(s, 1)
        eg = jnp.exp(gc)

        qs = qn * (hd ** -0.5)
        kb = kn * beta
        g_s[hh] = jnp.broadcast_to(gc, (s, LANES))
        k_s[hh] = kn.astype(BF16)
        kq_s[hh, :, :CHUNK, :] = chunked(kb.astype(BF16))
        kq_s[hh, :, CHUNK:, :] = chunked(qs.astype(BF16))
        kv_s[hh, :, :hd] = (kb * eg).astype(BF16)
        kv_s[hh, :, hd:] = (vv * beta).astype(BF16)
        wq_s[hh, :, CHUNK:, :] = chunked((qs * eg).astype(BF16))
        kd_s[hh] = (kn * jnp.exp(g_last - gc)).astype(BF16)
        st_s[hh] = jnp.zeros((hd, hd), F32)

    ri = lax.broadcasted_iota(jnp.int32, (CHUNK, CHUNK), 0)
    ci = lax.broadcasted_iota(jnp.int32, (CHUNK, CHUNK), 1)
    eye = ri == ci
    eye_f = jnp.where(eye, 1.0, 0.0).astype(F32)
    n_factors = CHUNK.bit_length() - 2

    heads = range(GDN_GROUP)

    def prep(it):
        items = [(hh, it * GDN_PREP_CHUNKS + c) for c in range(GDN_PREP_CHUNKS) for hh in heads]
        sls = [pl.ds(pl.multiple_of(n * CHUNK, CHUNK), CHUNK) for _, n in items]
        rs = [_dot_nt(kq_s[hh, n], k_s[hh, sl, :]) for (hh, n), sl in zip(items, sls)]
        yield
        ms, ps = [], []
        for (hh, n), sl, r in zip(items, sls, rs):
            g_col = g_s[hh, sl, :][:, :CHUNK]
            g_row = jnp.sum(jnp.where(eye, g_col, 0.0), axis=0, keepdims=True)
            decay = jnp.exp(jnp.where(ri >= ci, g_col - g_row, -jnp.inf))
            a = jnp.where(ri > ci, r[:CHUNK] * decay, 0.0)
            qk_s[hh, sl, :] = (r[CHUNK:] * decay).astype(BF16)
            ms.append(a)
            ps.append(eye_f - a)
        ms = [_dot_r(a, a) for a in ms]
        yield
        for _ in range(n_factors - 1):
            rs = [_dot_r(jnp.concatenate([m, p], axis=0), m) for m, p in zip(ms, ps)]
            ms = [r[:CHUNK] for r in rs]
            ps = [p + r[CHUNK:] for p, r in zip(ps, rs)]
            yield
        rs = [_dot_r(p, m) for m, p in zip(ms, ps)]
        yield
        wus = [_dot((p + r).astype(BF16), kv_s[hh, sl, :]) for (hh, _), sl, p, r in zip(items, sls, ps, rs)]
        for (hh, n), sl, wu in zip(items, sls, wus):
            wq_s[hh, n, :CHUNK, :] = wu[:, :hd].astype(BF16)
            u_s[hh, sl, :] = wu[:, hd:]
        yield

    def scan(n):
        r0 = pl.multiple_of(n * CHUNK, CHUNK)
        sl = pl.ds(r0, CHUNK)
        states = [st_s[hh] for hh in heads]
        rs = [_dot(wq_s[hh, n], states[hh].astype(BF16)) for hh in heads]
        yield
        v_nbs = [(u_s[hh, sl, :] - rs[hh][:CHUNK]).astype(BF16) for hh in heads]
        os = [rs[hh][CHUNK:] + _dot(qk_s[hh, sl, :], v_nbs[hh]) for hh in heads]
        kvs = [_dot_tn(kd_s[hh, sl, :], v_nbs[hh]) for hh in heads]
        yield
        for hh in heads:
            decay_last = jnp.exp(g_s[hh, pl.ds(r0 + CHUNK - 1, 1), :])
            st_s[hh] = states[hh] * decay_last + kvs[hh]
            z = z_ref[sl, hh * hd:(hh + 1) * hd].astype(F32)
            o_ref[sl, hh * hd:(hh + 1) * hd] = (_rms(os[hh], on_ref[...]) * (z * jax.nn.sigmoid(z))).astype(o_ref.dtype)

    n_groups = n_chunks // GDN_PREP_CHUNKS
    for _ in prep(0):
        pass

    def body(it, carry):
        prep_stages = prep((it + 1) % n_groups)
        for c in range(GDN_PREP_CHUNKS):
            for _ in scan(it * GDN_PREP_CHUNKS + c):
                next(prep_stages, None)
        for _ in prep_stages:
            pass
        return carry

    lax.fori_loop(0, n_groups, body, 0)


def _gdn(qk_n, v_n, z_tile, beta_all, gc_all, out_norm, b, s):
    hd = DN_HEAD_DIM
    gw = hd * GDN_GROUP
    n_groups = DN_HEADS // GDN_GROUP
    n_chunks = s // CHUNK
    col = lambda k: pl.BlockSpec((s, gw), lambda bi, g: (bi, k * n_groups + g))
    gates = pl.BlockSpec((s, LANES), lambda bi, g: (bi, 0))
    per_head = lambda shape, dt: pltpu.VMEM((GDN_GROUP,) + shape, dt)
    return pl.pallas_call(
        _gdn_kernel,
        grid=(b, n_groups),
        in_specs=[
            col(0), col(1), col(0), col(1),
            gates, gates,
            _resident((1, hd)),
        ],
        out_specs=pl.BlockSpec((s, gw), lambda bi, g: (bi, g)),
        out_shape=jax.ShapeDtypeStruct((b * s, DN_WIDTH), BF16),
        scratch_shapes=[
            per_head((s, LANES), F32),
            per_head((s, hd), BF16),
            per_head((n_chunks, 2 * CHUNK, hd), BF16),
            per_head((s, 2 * hd), BF16),
            per_head((n_chunks, 2 * CHUNK, hd), BF16),
            per_head((s, hd), BF16),
            per_head((s, hd), F32),
            per_head((s, CHUNK), BF16),
            per_head((hd, hd), F32),
        ],
        compiler_params=_cparams("parallel", "arbitrary"),
        name="gdn",
    )(qk_n, qk_n, v_n, z_tile, beta_all, gc_all, out_norm)


ATTN_LOOKAHEAD = 1


def _attn_kernel(q_ref, k_ref, v_ref, o_ref, v1_s, *, tq):
    s = q_ref.shape[2]
    v1_s[:, :MLA_V] = v_ref[0, 0]
    v1_s[:, MLA_V:] = jnp.ones((s, MLA_V), BF16)
    ri = lax.broadcasted_iota(jnp.int32, (tq, tq), 0) // CHUNK
    ci = lax.broadcasted_iota(jnp.int32, (tq, tq), 1) // CHUNK
    allowed = ci <= ri
    n_q = s // tq

    def scores(qi):
        lo, hi = qi * tq, (qi + 1) * tq
        q = q_ref[0, 0, lo:hi, :]
        s_d = jnp.where(allowed, _dot_nt(q, k_ref[0, 0, lo:hi, :]), -jnp.inf)
        s_o = _dot_nt(q, k_ref[0, 0, :lo, :]) if qi > 0 else None
        return s_d, s_o

    pending = [scores(qi) for qi in range(min(ATTN_LOOKAHEAD, n_q))]
    for qi in range(n_q):
        lo, hi = qi * tq, (qi + 1) * tq
        if qi + ATTN_LOOKAHEAD < n_q:
            pending.append(scores(qi + ATTN_LOOKAHEAD))
        s_d, s_o = pending.pop(0)
        m = jnp.max(s_d, axis=-1, keepdims=True)
        if s_o is not None:
            m = jnp.maximum(m, jnp.max(s_o, axis=-1, keepdims=True))
        acc = _dot(jnp.exp2(s_d - m).astype(BF16), v1_s[lo:hi, :])
        if s_o is not None:
            acc = acc + _dot(jnp.exp2(s_o - m).astype(BF16), v1_s[:lo, :])
        o_ref[lo:hi, :] = (acc[:, :MLA_V] / acc[:, MLA_V:]).astype(o_ref.dtype)


def _mla_attn(q, k, v):
    b, nh, s, _ = q.shape
    tq = min(256, s)
    spec = lambda w: pl.BlockSpec((1, 1, s, w), lambda bi, h: (bi, h, 0, 0))
    return pl.pallas_call(
        functools.partial(_attn_kernel, tq=tq),
        grid=(b, nh),
        in_specs=[spec(MLA_QK), spec(MLA_QK), spec(MLA_V)],
        out_specs=pl.BlockSpec((s, MLA_V), lambda bi, h: (bi, h)),
        out_shape=jax.ShapeDtypeStruct((b * s, MLA_WIDTH), BF16),
        scratch_shapes=[pltpu.VMEM((s, 2 * MLA_V), BF16)],
        compiler_params=_cparams("parallel", "parallel"),
        name="mla_attn",
    )(q, k, v)


def _merge_kernel(x_ref, oa_ref, ob_ref, ga_ref, gb_ref, wa_ref, wb_ref, wo_ref, o_ref):
    ya = jax.nn.sigmoid(ga_ref[...].astype(F32)) * _dot(oa_ref[...], wa_ref[...])
    yb = jax.nn.sigmoid(gb_ref[...].astype(F32)) * _dot(ob_ref[...], wb_ref[...])
    o_ref[...] = x_ref[...] + _dot((ya + yb).astype(BF16), wo_ref[...])


def _merge(x2, o_a, o_b, gate_a, gate_b, w_a, w_b, w_o):
    t = x2.shape[0]
    tm = min(512, t)
    row = lambda w, blk: pl.BlockSpec((tm, w), lambda i: (i, blk))
    return pl.pallas_call(
        _merge_kernel,
        grid=(t // tm,),
        in_specs=[
            row(D_MODEL, 0), row(DN_WIDTH, 0), row(MLA_WIDTH, 0), row(D_MODEL, 0), row(D_MODEL, 0),
            _resident((DN_WIDTH, D_MODEL)), _resident((MLA_WIDTH, D_MODEL)), _resident((D_MODEL, D_MODEL)),
        ],
        out_specs=row(D_MODEL, 0),
        out_shape=jax.ShapeDtypeStruct((t, D_MODEL), F32),
        compiler_params=_cparams("parallel"),
        name="merge",
    )(x2, o_a, o_b, gate_a, gate_b, w_a, w_b, w_o)


def _mlp_kernel(x_ref, gm_ref, up_ref, dn_ref, o_ref, hm_ref):
    @pl.when(pl.program_id(1) == 0)
    def _():
        x = x_ref[...]
        hm_ref[...] = _rms(x, gm_ref[...]).astype(BF16)
        o_ref[...] = x

    a = _dot(hm_ref[...], up_ref[...])
    o_ref[...] += _dot(jnp.square(jnp.maximum(a, 0.0)).astype(BF16), dn_ref[...])


def _mlp(x1, gm, w_up, w_dn):
    t = x1.shape[0]
    tm = min(1024, t)
    tf = 1024
    return pl.pallas_call(
        _mlp_kernel,
        grid=(t // tm, D_FF // tf),
        in_specs=[
            pl.BlockSpec((tm, D_MODEL), lambda i, f: (i, 0)),
            _resident((1, D_MODEL)),
            pl.BlockSpec((D_MODEL, tf), lambda i, f: (0, f)),
            pl.BlockSpec((tf, D_MODEL), lambda i, f: (f, 0)),
        ],
        out_specs=pl.BlockSpec((tm, D_MODEL), lambda i, f: (i, 0)),
        out_shape=jax.ShapeDtypeStruct((t, D_MODEL), F32),
        scratch_shapes=[pltpu.VMEM((tm, D_MODEL), BF16)],
        compiler_params=_cparams("parallel", "arbitrary"),
        name="mlp",
    )(x1, gm, w_up, w_dn)


def _ple_kernel(x_ref, gp_ref, wg_ref, p_ref, wp_ref, o_ref):
    x = x_ref[...]
    gate = jax.nn.sigmoid(_dot(_rms(x, gp_ref[...]).astype(BF16), wg_ref[...]))
    o_ref[...] = x + gate * _dot(p_ref[...].astype(BF16), wp_ref[...])


def _ple(x2, gp, w_gate, p2, w_ple):
    t = x2.shape[0]
    tm = min(1024, t)
    row = lambda w: pl.BlockSpec((tm, w), lambda i: (i, 0))
    return pl.pallas_call(
        _ple_kernel,
        grid=(t // tm,),
        in_specs=[row(D_MODEL), _resident((1, D_MODEL)), _resident((D_MODEL, D_MODEL)), row(PLE_DIM),
                  _resident((PLE_DIM, D_MODEL))],
        out_specs=row(D_MODEL),
        out_shape=jax.ShapeDtypeStruct((t, D_MODEL), F32),
        compiler_params=_cparams("parallel"),
        name="ple",
    )(x2, gp, w_gate, p2, w_ple)


def _regroup_w_in(w):
    d = w.shape[0]
    mq = w[:, _OFF_MQ:_OFF_CKV].reshape(d, MLA_HEADS, MLA_QK)
    tiles = (
        jnp.concatenate([
            mq[:, :, :MLA_NOPE].reshape(d, MLA_HEADS * MLA_NOPE),
            mq[:, :, MLA_NOPE:].reshape(d, MLA_HEADS * MLA_ROPE),
            w[:, _OFF_CKV:_OFF_KR],
        ], axis=1),
        w[:, _OFF_Q:_OFF_V],
        w[:, _OFF_V:_OFF_BETA],
        w[:, _OFF_GA:_OFF_GB],
        w[:, _OFF_GB:D_IN],
    )
    tail = jnp.concatenate([
        w[:, _OFF_KR:_OFF_GA],
        w[:, _OFF_BETA:_OFF_MQ],
        jnp.zeros((d, LANES - MLA_ROPE - 2 * DN_HEADS), w.dtype),
    ], axis=1)
    return tuple(tile.astype(BF16) for tile in tiles), tail.astype(BF16)


def kernel(x, p, positions, mix_norm, w_in, conv_w, dt_bias, a_log, dn_out_norm, ckv_norm, w_kv_up, q_nope_norm,
           q_rope_norm, k_nope_norm, k_rope_norm, w_branch_a, w_branch_b, w_out, mlp_norm, w_mlp_up, w_mlp_down,
           ple_norm, w_ple_gate, w_ple):
    b, s, d = x.shape
    t = b * s
    depth = w_in.shape[0]
    half = MLA_ROPE // 2
    inv = ROPE_BASE ** (-jnp.arange(half, dtype=F32) / half)
    inv64 = jnp.concatenate([inv, inv, inv, inv]).reshape(1, 2 * MLA_ROPE)
    pos = positions.reshape(t, 1)
    row = lambda a: a.reshape(1, -1).astype(F32)

    x2 = x.reshape(t, d)
    for i in range(depth):
        w_main, w_tail = _regroup_w_in(w_in[i])
        pr = _in_proj(
            x2, row(mix_norm[i]), w_main, w_tail, conv_w[i], a_log[i], dt_bias[i], pos, inv64, row(q_nope_norm[i]),
            row(q_rope_norm[i]), row(ckv_norm[i]), row(k_nope_norm[i]), row(k_rope_norm[i]), w_kv_up[i].astype(BF16),
            b, s)
        o_a = _gdn(pr["qk_n"], pr["v_n"], pr["z_tile"], pr["beta_all"], pr["gc_all"], row(dn_out_norm[i]), b, s)
        o_b = _mla_attn(pr["q_b"], pr["k_b"], pr["v_b"])
        x1 = _merge(x2, o_a, o_b, pr["gate_a"], pr["gate_b"], w_branch_a[i].astype(BF16),
                    w_branch_b[i].astype(BF16), w_out[i].astype(BF16))
        x2 = _mlp(x1, row(mlp_norm[i]), w_mlp_up[i].astype(BF16), w_mlp_down[i].astype(BF16))
        x2 = _ple(x2, row(ple_norm[i]), w_ple_gate[i].astype(BF16), p[i].reshape(t, PLE_DIM), w_ple[i].astype(BF16))
    return x2.reshape(b, s, d)
```
